```python
import math
import jax, jax.numpy as jnp
from jax import lax
import numpy as np

D_MODEL = 1024
BATCH = 2
SEQ = 16384
DEPTH = 4

D_RNN = 1024
RNN_BLOCKS = 16
RNN_BW = D_RNN // RNN_BLOCKS
CONV_W = 4
LRU_C = 8.0
N_HEADS = 8
HEAD_DIM = 128
D_ATT = N_HEADS * HEAD_DIM
KV_RANK = 256
IDX_HEADS = 8
IDX_DIM = 64
TOPK_MAX = 256
Q_BLOCK = 128
N_BUCKETS = 32
MAX_DIST = 128
N_EXPERTS = 16
N_GROUPS = 4
EXPERTS_PER_GROUP = N_EXPERTS // N_GROUPS
TOP_K_EXPERTS = 2
D_EXPERT = 512
ALPHA = (2 * DEPTH) ** 0.25
BETA = (8 * DEPTH) ** -0.25
LN_EPS = 1e-5
RMS_EPS = 1e-6
SPLITS = (D_RNN, D_RNN, D_ATT, KV_RANK, IDX_HEADS * IDX_DIM, IDX_DIM, IDX_HEADS, D_MODEL, D_MODEL)
D_IN = sum(SPLITS)
SPLIT_AT = tuple(int(v) for v in np.cumsum(SPLITS)[:-1])

kernel_name = "hybrid_rglru_dsa_groupmoe_deepnorm"


def layer_norm(x, g, b):
    xf = x.astype(jnp.float32)
    mu = jnp.mean(xf, axis=-1, keepdims=True)
    var = jnp.mean(jnp.square(xf - mu), axis=-1, keepdims=True)
    return ((xf - mu) * lax.rsqrt(var + LN_EPS) * g + b).astype(x.dtype)


def rms_norm(x, g):
    xf = x.astype(jnp.float32)
    return (xf * lax.rsqrt(jnp.mean(jnp.square(xf), axis=-1, keepdims=True) + RMS_EPS) * g).astype(x.dtype)


def causal_depthwise_conv(x, w, b):
    S = x.shape[1]
    xp = jnp.pad(x, ((0, 0), (CONV_W - 1, 0), (0, 0)))
    return b + sum(w[k] * xp[:, k:k + S] for k in range(CONV_W))


def rg_lru(x, w_a, b_a, w_x, b_x, lam):
    B, S, _ = x.shape
    xb = x.reshape(B, S, RNN_BLOCKS, RNN_BW)
    r = jax.nn.sigmoid(jnp.einsum('bsgi,gij->bsgj', xb, w_a).reshape(B, S, D_RNN) + b_a)
    i = jax.nn.sigmoid(jnp.einsum('bsgi,gij->bsgj', xb, w_x).reshape(B, S, D_RNN) + b_x)
    log_a = (-LRU_C * r.astype(jnp.float32)) * jax.nn.softplus(-lam.astype(jnp.float32))
    a = jnp.exp(log_a)
    u = jnp.sqrt(-jnp.expm1(2.0 * log_a)) * (i * x).astype(jnp.float32)

    def combine(left, right):
        a_l, b_l = left
        a_r, b_r = right
        return a_l * a_r, a_r * b_l + b_r

    _, h = lax.associative_scan(combine, (a, u), axis=1)
    return h.astype(x.dtype)


def t5_bucket(dist):
    max_exact = N_BUCKETS // 2
    d = jnp.maximum(dist, 0)
    large = max_exact + (jnp.log(jnp.maximum(d, 1).astype(jnp.float32) / max_exact)
                         / math.log(MAX_DIST / max_exact) * (N_BUCKETS - max_exact)).astype(jnp.int32)
    large = jnp.minimum(large, N_BUCKETS - 1)
    return jnp.where(d < max_exact, d, large)


def dsa_attention(q, c_kv, q_idx, k_idx, w_idx, w_uk, w_uv, rel_bias):
    B, S = c_kv.shape[:2]
    n_keep = min(TOPK_MAX, S // 4)
    nb = S // Q_BLOCK
    q_lat = jnp.einsum('bshd,hcd->bshc', q, w_uk) * (HEAD_DIM ** -0.5)
    key_pos = jnp.arange(S, dtype=jnp.int32)

    def blocks(a):
        return jnp.swapaxes(a.reshape(B, nb, Q_BLOCK, *a.shape[2:]), 0, 1)

    def one_block(args):
        blk, ql, qi, wi = args
        t = blk * Q_BLOCK + jnp.arange(Q_BLOCK, dtype=jnp.int32)
        sc = jax.nn.relu(jnp.einsum('bqhd,bsd->bqhs', qi, k_idx))
        sc = jnp.einsum('bqhs,bqh->bqs', sc, wi).astype(jnp.float32)
        causal = key_pos[None, :] <= t[:, None]
        sc = jnp.where(causal[None], sc, -jnp.inf)
        _, idx = lax.top_k(sc, n_keep)
        valid = idx <= t[None, :, None]
        kv = jax.vmap(lambda c, ix: c[ix])(c_kv, idx)
        logits = jnp.einsum('bqhc,bqkc->bhqk', ql, kv).astype(jnp.float32)
        bias = rel_bias[t5_bucket(t[None, :, None] - idx)]
        logits = logits + jnp.moveaxis(bias, -1, 1).astype(jnp.float32)
        logits = jnp.where(valid[:, None], logits, -1e30)
        p = jax.nn.softmax(logits, axis=-1).astype(kv.dtype)
        return jnp.einsum('bhqk,bqkc->bqhc', p, kv)

    o_lat = lax.map(one_block, (jnp.arange(nb, dtype=jnp.int32), blocks(q_lat), blocks(q_idx), blocks(w_idx)))
    o_lat = jnp.swapaxes(o_lat, 0, 1).reshape(B, S, N_HEADS, KV_RANK)
    o = jnp.einsum('bshc,hcd->bshd', o_lat, w_uv)
    return o.reshape(B, S, D_ATT)


def token_mixer(x, w_in, conv_w, conv_b, lru_wa, lru_ba, lru_wx, lru_bx, lru_lambda,
                kv_norm, w_uk, w_uv, proj_rnn, proj_att, w_out, rel_bias):
    B, S, _ = x.shape
    h = x @ w_in
    x_rnn, g_rnn, q, c_kv, q_idx, k_idx, w_idx, gate_rnn, gate_att = jnp.split(h, SPLIT_AT, axis=-1)
    xr = causal_depthwise_conv(x_rnn, conv_w, conv_b)
    y_rnn = rg_lru(xr, lru_wa, lru_ba, lru_wx, lru_bx, lru_lambda) * jax.nn.gelu(g_rnn)
    y_att = dsa_attention(q.reshape(B, S, N_HEADS, HEAD_DIM), rms_norm(c_kv, kv_norm),
                          q_idx.reshape(B, S, IDX_HEADS, IDX_DIM) * (IDX_DIM ** -0.5), k_idx,
                          w_idx * (IDX_HEADS ** -0.5), w_uk, w_uv, rel_bias)
    merged = jax.nn.sigmoid(gate_rnn) * (y_rnn @ proj_rnn) + jax.nn.sigmoid(gate_att) * (y_att @ proj_att)
    return merged @ w_out


def grouped_moe(x, w_router, router_bias, w_gate, w_up, w_down):
    B, S, D = x.shape
    tok = x.reshape(B * S, D)
    aff = jax.nn.sigmoid((tok @ w_router).astype(jnp.float32))
    sel = aff + router_bias.astype(jnp.float32)
    grp = sel.reshape(-1, N_GROUPS, EXPERTS_PER_GROUP)
    grp_score = jnp.sum(lax.top_k(grp, TOP_K_EXPERTS)[0], axis=-1)
    g_best = jnp.argmax(grp_score, axis=-1)
    in_group = (jnp.arange(N_EXPERTS) // EXPERTS_PER_GROUP)[None, :] == g_best[:, None]
    _, e_idx = lax.top_k(jnp.where(in_group, sel, -jnp.inf), TOP_K_EXPERTS)
    g = jnp.take_along_axis(aff, e_idx, axis=-1)
    g = g / jnp.sum(g, axis=-1, keepdims=True)
    combine = jnp.sum(jax.nn.one_hot(e_idx, N_EXPERTS, dtype=jnp.float32) * g[..., None], axis=1)
    out = jnp.zeros_like(tok)
    for e in range(N_EXPERTS):
        he = jax.nn.silu(tok @ w_gate[e]) * (tok @ w_up[e])
        out = out + combine[:, e:e + 1].astype(tok.dtype) * (he @ w_down[e])
    return out.reshape(B, S, D)


def setup_inputs(seed: int = 0) -> dict:
    key = jax.random.key(seed)
    ks = jax.random.split(key, 24)
    f32 = jnp.float32
    L = DEPTH

    def nrm(k, shape, scale):
        return jax.random.normal(k, shape, f32) * scale

    u = jax.random.uniform(ks[9], (L, D_RNN), f32, 0.9, 0.999)
    a0 = u ** (1.0 / LRU_C)
    lru_lambda = jnp.log(a0) - jnp.log1p(-a0)
    return {
        "x": nrm(ks[0], (BATCH, SEQ, D_MODEL), 1.0),
        "w_in": nrm(ks[1], (L, D_MODEL, D_IN), D_MODEL ** -0.5),
        "conv_w": nrm(ks[2], (L, CONV_W, D_RNN), CONV_W ** -0.5),
        "conv_b": nrm(ks[3], (L, D_RNN), 0.01),
        "lru_wa": nrm(ks[4], (L, RNN_BLOCKS, RNN_BW, RNN_BW), RNN_BW ** -0.5),
        "lru_ba": nrm(ks[5], (L, D_RNN), 0.01),
        "lru_wx": nrm(ks[6], (L, RNN_BLOCKS, RNN_BW, RNN_BW), RNN_BW ** -0.5),
        "lru_bx": nrm(ks[7], (L, D_RNN), 0.01),
        "lru_lambda": lru_lambda,
        "kv_norm": 1.0 + nrm(ks[8], (L, KV_RANK), 0.02),
        "w_uk": nrm(ks[10], (L, N_HEADS, KV_RANK, HEAD_DIM), KV_RANK ** -0.5),
        "w_uv": nrm(ks[11], (L, N_HEADS, KV_RANK, HEAD_DIM), KV_RANK ** -0.5),
        "proj_rnn": nrm(ks[12], (L, D_RNN, D_MODEL), D_RNN ** -0.5),
        "proj_att": nrm(ks[13], (L, D_ATT, D_MODEL), D_ATT ** -0.5),
        "w_out": nrm(ks[14], (L, D_MODEL, D_MODEL), BETA * D_MODEL ** -0.5),
        "ln1_g": 1.0 + nrm(ks[15], (L, D_MODEL), 0.02),
        "ln1_b": nrm(ks[16], (L, D_MODEL), 0.01),
        "w_router": nrm(ks[17], (D_MODEL, N_EXPERTS), D_MODEL ** -0.5),
        "router_bias": nrm(ks[18], (N_EXPERTS,), 0.01),
        "exp_w_gate": nrm(ks[19], (L, N_EXPERTS, D_MODEL, D_EXPERT), D_MODEL ** -0.5),
        "exp_w_up": nrm(ks[20], (L, N_EXPERTS, D_MODEL, D_EXPERT), D_MODEL ** -0.5),
        "exp_w_down": nrm(ks[21], (L, N_EXPERTS, D_EXPERT, D_MODEL), BETA * D_EXPERT ** -0.5),
        "ln2_g": 1.0 + nrm(ks[22], (L, D_MODEL), 0.02),
        "ln2_b": nrm(ks[23], (L, D_MODEL), 0.01),
        "rel_bias": nrm(jax.random.fold_in(key, 99), (N_BUCKETS, N_HEADS), 0.1),
    }


def reference(x, w_in, conv_w, conv_b, lru_wa, lru_ba, lru_wx, lru_bx, lru_lambda, kv_norm,
              w_uk, w_uv, proj_rnn, proj_att, w_out, ln1_g, ln1_b, w_router, router_bias,
              exp_w_gate, exp_w_up, exp_w_down, ln2_g, ln2_b, rel_bias):
    for l in range(DEPTH):
        mix = token_mixer(x, w_in[l], conv_w[l], conv_b[l], lru_wa[l], lru_ba[l], lru_wx[l], lru_bx[l],
                          lru_lambda[l], kv_norm[l], w_uk[l], w_uv[l], proj_rnn[l], proj_att[l], w_out[l],
                          rel_bias)
        x = layer_norm(ALPHA * x + mix, ln1_g[l], ln1_b[l])
        ffn = grouped_moe(x, w_router, router_bias, exp_w_gate[l], exp_w_up[l], exp_w_down[l])
        x = layer_norm(ALPHA * x + ffn, ln2_g[l], ln2_b[l])
    return x
```

```python
import dataclasses
import functools
import math

import jax
import jax.numpy as jnp
from jax import lax
from jax.experimental import pallas as pl
from jax.experimental.pallas import tpu as pltpu
from jax.experimental.pallas import tpu_sc as plsc

D_MODEL = 1024
D_RNN = 1024
RNN_BLOCKS = 16
RNN_BW = D_RNN // RNN_BLOCKS
CONV_W = 4
LRU_C = 8.0
N_HEADS = 8
HEAD_DIM = 128
D_ATT = N_HEADS * HEAD_DIM
KV_RANK = 256
IDX_HEADS = 8
IDX_DIM = 64
TOPK_MAX = 256
N_BUCKETS = 32
MAX_DIST = 128
N_EXPERTS = 16
N_GROUPS = 4
EXPERTS_PER_GROUP = N_EXPERTS // N_GROUPS
D_EXPERT = 512
LN_EPS = 1e-5
RMS_EPS = 1e-6

COL_XRNN = 0
COL_GRNN = 1024
COL_Q = 2048
COL_GATE_RNN = 3072
COL_GATE_ATT = 4096
COL_QIDX = 5120
COL_CKV = 5632
COL_KIDX = 5888
D_IN_PAD = 6144

SC_CORES = 2
SC_SUBCORES = 16
SC_LANES = 16
SC_WORKERS = SC_CORES * SC_SUBCORES
SC_MAX_INDEX_ROW = 128

INT_MIN = -2 ** 31
VMEM_LIMIT = 56 * 1024 * 1024


def _cparams(*sem):
    return pltpu.CompilerParams(dimension_semantics=sem, vmem_limit_bytes=VMEM_LIMIT)


def _layer_norm(v, g, b):
    mu = jnp.mean(v, axis=-1, keepdims=True)
    var = jnp.mean(jnp.square(v - mu), axis=-1, keepdims=True)
    return (v - mu) * lax.rsqrt(var + LN_EPS) * g + b


def _inproj_kernel(x_ref, w_ref, o_ref, xb_ref):
    @pl.when(pl.program_id(1) == 0)
    def _():
        xb_ref[...] = x_ref[...].astype(jnp.bfloat16)

    o_ref[...] = jnp.dot(xb_ref[...], w_ref[...], preferred_element_type=jnp.float32)


def _inproj(x, w, tm=1024, tn=512):
    T, K = x.shape
    N = w.shape[1]
    return pl.pallas_call(
        _inproj_kernel,
        grid=(T // tm, N // tn),
        in_specs=[pl.BlockSpec((tm, K), lambda i, j: (i, 0)),
                  pl.BlockSpec((K, tn), lambda i, j: (0, j))],
        out_specs=pl.BlockSpec((tm, tn), lambda i, j: (i, j)),
        out_shape=jax.ShapeDtypeStruct((T, N), jnp.float32),
        scratch_shapes=[pltpu.VMEM((tm, K), jnp.bfloat16)],
        compiler_params=_cparams("parallel", "arbitrary"),
        name="inproj",
    )(x, w)


def _rglru_kernel(x_ref, g_ref, cw_ref, cb_ref, wa_ref, ba_ref, wx_ref, bx_ref, lam_ref,
                  o_ref, prev_ref, carry_ref, a_ref, u_ref):
    ts = x_ref.shape[0]

    @pl.when(pl.program_id(1) == 0)
    def _():
        prev_ref[...] = jnp.zeros_like(prev_ref)
        carry_ref[...] = jnp.zeros_like(carry_ref)

    x = x_ref[...]
    xe = jnp.concatenate([prev_ref[...], x], axis=0)
    xr = cb_ref[...] + sum(cw_ref[k:k + 1, :] * xe[5 + k:5 + k + ts, :] for k in range(CONV_W))
    prev_ref[...] = x[ts - 8:, :]

    xb = xr.astype(jnp.bfloat16)
    nt = D_RNN // 256
    ra = jnp.concatenate([jnp.dot(xb[:, c * 256:(c + 1) * 256], wa_ref[c],
                                  preferred_element_type=jnp.float32) for c in range(nt)], axis=1)
    rx = jnp.concatenate([jnp.dot(xb[:, c * 256:(c + 1) * 256], wx_ref[c],
                                  preferred_element_type=jnp.float32) for c in range(nt)], axis=1)
    r = jax.nn.sigmoid(ra + ba_ref[...])
    gi = jax.nn.sigmoid(rx + bx_ref[...])
    z = -lam_ref[...]
    softplus = jnp.maximum(z, 0.0) + jnp.log(1.0 + jnp.exp(-jnp.abs(z)))
    log_a = (-LRU_C * r) * softplus
    a_ref[...] = jnp.exp(log_a)
    u_ref[...] = jnp.sqrt(1.0 - jnp.exp(2.0 * log_a)) * (gi * xr)

    row = lax.broadcasted_iota(jnp.int32, (8, D_RNN), 0)

    def group(gidx, carry):
        r0 = pl.multiple_of(gidx * 8, 8)
        a8 = a_ref[pl.ds(r0, 8), :]
        u8 = u_ref[pl.ds(r0, 8), :]
        for d in (1, 2, 4):
            keep = row >= d
            a_sh = pltpu.roll(a8, d, 0)
            u_sh = pltpu.roll(u8, d, 0)
            u8 = jnp.where(keep, a8 * u_sh + u8, u8)
            a8 = jnp.where(keep, a8 * a_sh, a8)
        h8 = a8 * carry + u8
        u_ref[pl.ds(r0, 8), :] = h8
        return h8[7:8, :]

    carry_ref[...] = lax.fori_loop(0, ts // 8, group, carry_ref[...], unroll=4)
    o_ref[...] = (u_ref[...] * jax.nn.gelu(g_ref[...])).astype(o_ref.dtype)


def _rglru(h_all, B, S, cw, cb, wa, ba, wx, bx, lam, ts=256):
    nblk = S // ts
    row = lambda b, i: b * nblk + i
    vec = pl.BlockSpec((1, D_RNN), lambda b, i: (0, 0))
    tile = pl.BlockSpec((D_RNN // 256, 256, 256), lambda b, i: (0, 0, 0))
    return pl.pallas_call(
        _rglru_kernel,
        grid=(B, nblk),
        in_specs=[pl.BlockSpec((ts, D_RNN), lambda b, i: (row(b, i), COL_XRNN // D_RNN)),
                  pl.BlockSpec((ts, D_RNN), lambda b, i: (row(b, i), COL_GRNN // D_RNN)),
                  pl.BlockSpec((CONV_W, D_RNN), lambda b, i: (0, 0)),
                  vec, tile, vec, tile, vec, vec],
        out_specs=pl.BlockSpec((ts, D_RNN), lambda b, i: (row(b, i), 0)),
        out_shape=jax.ShapeDtypeStruct((B * S, D_RNN), jnp.bfloat16),
        scratch_shapes=[pltpu.VMEM((8, D_RNN), jnp.float32),
                        pltpu.VMEM((1, D_RNN), jnp.float32),
                        pltpu.VMEM((ts, D_RNN), jnp.float32),
                        pltpu.VMEM((ts, D_RNN), jnp.float32)],
        compiler_params=_cparams("arbitrary", "arbitrary"),
        name="rglru",
    )(h_all, h_all, cw, cb, wa, ba, wx, bx, lam)


def _kvpack_kernel(c_ref, g_ref, o_ref):
    c = c_ref[...]
    cn = c * lax.rsqrt(jnp.mean(jnp.square(c), axis=-1, keepdims=True) + RMS_EPS) * g_ref[...]
    cb = cn.astype(jnp.bfloat16).astype(jnp.float32)
    half = KV_RANK // 2
    lo = lax.bitcast_convert_type(cb[:, :half], jnp.int32)
    hi = lax.bitcast_convert_type(cb[:, half:], jnp.int32)
    o_ref[...] = (hi & jnp.int32(-65536)) | lax.shift_right_logical(lo, 16)


def _kvpack(h_all, kv_norm, ts=1024):
    T = h_all.shape[0]
    return pl.pallas_call(
        _kvpack_kernel,
        grid=(T // ts,),
        in_specs=[pl.BlockSpec((ts, KV_RANK), lambda i: (i, COL_CKV // KV_RANK)),
                  pl.BlockSpec((1, KV_RANK), lambda i: (0, 0))],
        out_specs=pl.BlockSpec((ts, KV_RANK // 2), lambda i: (i, 0)),
        out_shape=jax.ShapeDtypeStruct((T, KV_RANK // 2), jnp.int32),
        compiler_params=_cparams("parallel"),
        name="kvpack",
    )(h_all, kv_norm)


def _unpack_kv(w):
    lo = lax.bitcast_convert_type(lax.shift_left(w, 16), jnp.float32)
    hi = lax.bitcast_convert_type(w & jnp.int32(-65536), jnp.float32)
    return jnp.concatenate([lo, hi], axis=-1).astype(jnp.bfloat16)


def _qlat_kernel(q_ref, w_ref, o_ref):
    q = q_ref[...].astype(jnp.bfloat16)
    for h in range(N_HEADS):
        o_ref[:, h, :] = jnp.dot(q[:, h * HEAD_DIM:(h + 1) * HEAD_DIM], w_ref[h],
                                 preferred_element_type=jnp.float32) * (HEAD_DIM ** -0.5)


def _qlat(h_all, w_ukT, tm=256):
    T = h_all.shape[0]
    return pl.pallas_call(
        _qlat_kernel,
        grid=(T // tm,),
        in_specs=[pl.BlockSpec((tm, D_ATT), lambda i: (i, COL_Q // D_ATT)),
                  pl.BlockSpec((N_HEADS, HEAD_DIM, KV_RANK), lambda i: (0, 0, 0))],
        out_specs=pl.BlockSpec((tm, N_HEADS, KV_RANK), lambda i: (i, 0, 0)),
        out_shape=jax.ShapeDtypeStruct((T, N_HEADS, KV_RANK), jnp.float32),
        compiler_params=_cparams("parallel"),
        name="qlat",
    )(h_all, w_ukT)


IDX_QB = 128
IDX_KC = 512


def _indexer_kernel(q_ref, kw_ref, k_ref, keys_ref, meta_ref, *, n_keep):
    S = keys_ref.shape[2]
    i = pl.program_id(1)
    n_causal = (i * IDX_QB + IDX_QB - 1) // IDX_KC + 1
    qs = [(q_ref[:, h * IDX_DIM:(h + 1) * IDX_DIM] * (IDX_DIM ** -0.5)).astype(jnp.bfloat16)
          for h in range(IDX_HEADS)]
    ws = [kw_ref[:, IDX_DIM + h:IDX_DIM + h + 1] * (IDX_HEADS ** -0.5) for h in range(IDX_HEADS)]
    t = i * IDX_QB + lax.broadcasted_iota(jnp.int32, (IDX_QB, IDX_KC), 0)
    lane = lax.broadcasted_iota(jnp.int32, (IDX_QB, IDX_KC), 1)

    def score_chunk(c, _):
        c0 = pl.multiple_of(c * IDX_KC, IDX_KC)
        kc = k_ref[0, pl.ds(c0, IDX_KC), :]
        sc = jnp.zeros((IDX_QB, IDX_KC), jnp.float32)
        for h in range(IDX_HEADS):
            d = lax.dot_general(qs[h], kc, (((1,), (1,)), ((), ())),
                                preferred_element_type=jnp.float32)
            sc = sc + jnp.maximum(d, 0.0) * ws[h]
        bits = lax.bitcast_convert_type(sc, jnp.int32)
        key = bits ^ (lax.shift_right_arithmetic(bits, 31) & jnp.int32(0x7FFFFFFF))
        keys_ref[0, :, pl.ds(c0, IDX_KC)] = jnp.where(c0 + lane <= t, key, jnp.int32(INT_MIN))
        return 0

    lax.fori_loop(0, n_causal, score_chunk, 0)

    def blank_chunk(c, _):
        c0 = pl.multiple_of(c * IDX_KC, IDX_KC)
        keys_ref[0, :, pl.ds(c0, IDX_KC)] = jnp.full((IDX_QB, IDX_KC), INT_MIN, jnp.int32)
        return 0

    lax.fori_loop(n_causal, S // IDX_KC, blank_chunk, 0)

    def count(pred):
        def chunk(c, acc):
            c0 = pl.multiple_of(c * IDX_KC, IDX_KC)
            hit = jnp.where(pred(keys_ref[0, :, pl.ds(c0, IDX_KC)]), 1, 0)
            for j in range(IDX_KC // 128):
                acc = acc + hit[:, j * 128:(j + 1) * 128]
            return acc

        acc = lax.fori_loop(0, n_causal, chunk, jnp.zeros((IDX_QB, 128), jnp.int32))
        return jnp.sum(acc, axis=1, keepdims=True)

    def bit_pass(p, res):
        cand = res | lax.shift_left(jnp.int32(1), 31 - p)
        cand_b = jnp.broadcast_to(cand ^ jnp.int32(INT_MIN), (IDX_QB, IDX_KC))
        n = count(lambda kk: kk >= cand_b)
        return jnp.where(n >= n_keep, cand, res)

    res = lax.fori_loop(0, 32, bit_pass, jnp.zeros((IDX_QB, 1), jnp.int32))
    thr = res ^ jnp.int32(INT_MIN)
    thr_b = jnp.broadcast_to(thr, (IDX_QB, IDX_KC))
    n_gt = count(lambda kk: kk > thr_b)
    half = lax.broadcasted_iota(jnp.int32, (IDX_QB, 128), 1) < 64
    meta_ref[0] = jnp.where(half, jnp.broadcast_to(thr, (IDX_QB, 128)),
                            jnp.broadcast_to(n_keep - n_gt, (IDX_QB, 128)))


def _indexer(h_all, kidx, B, S, n_keep):
    nblk = S // IDX_QB
    row = lambda b, i: b * nblk + i
    return pl.pallas_call(
        functools.partial(_indexer_kernel, n_keep=n_keep),
        grid=(B, nblk),
        in_specs=[pl.BlockSpec((IDX_QB, IDX_HEADS * IDX_DIM), lambda b, i: (row(b, i), COL_QIDX // 512)),
                  pl.BlockSpec((IDX_QB, 128), lambda b, i: (row(b, i), COL_KIDX // 128)),
                  pl.BlockSpec((1, S, IDX_DIM), lambda b, i: (b, 0, 0))],
        out_specs=[pl.BlockSpec((1, IDX_QB, S), lambda b, i: (b, i, 0)),
                   pl.BlockSpec((1, IDX_QB, 128), lambda b, i: (b, i, 0))],
        out_shape=[jax.ShapeDtypeStruct((B, S, S), jnp.int32),
                   jax.ShapeDtypeStruct((B, S, 128), jnp.int32)],
        compiler_params=_cparams("parallel", "arbitrary"),
        name="indexer",
    )(h_all, h_all, kidx)


SC_CHUNK = 1024


def _select_gather(keys, meta, table, B, S, n_keep):
    T = B * S
    W = table.shape[1]
    L = SC_LANES
    per_worker = T // SC_WORKERS
    n_idx_rows = n_keep // SC_MAX_INDEX_ROW
    mesh = plsc.VectorSubcoreMesh(core_axis_name="c", subcore_axis_name="s")
    cp = pltpu.CompilerParams()
    if "needs_layout_passes" in pltpu.CompilerParams.__dataclass_fields__:
        cp = dataclasses.replace(cp, needs_layout_passes=False)

    @functools.partial(
        pl.kernel, mesh=mesh, compiler_params=cp,
        out_type=(jax.ShapeDtypeStruct((T, n_idx_rows, SC_MAX_INDEX_ROW), jnp.int32),
                  jax.ShapeDtypeStruct((T, n_keep, W), jnp.int32)),
        scratch_types=[pltpu.VMEM((SC_CHUNK,), jnp.int32),
                       pltpu.VMEM((128,), jnp.int32),
                       pltpu.VMEM((n_idx_rows, SC_MAX_INDEX_ROW), jnp.int32),
                       pltpu.VMEM((n_keep, W), jnp.int32),
                       pltpu.SemaphoreType.DMA],
        name="select_gather",
    )
    def body(keys_hbm, meta_hbm, table_hbm, idx_hbm, rows_hbm, buf, meta_v, idx_v, rows_v, sem):
        wid = lax.axis_index("s") * SC_CORES + lax.axis_index("c")
        lane = lax.iota(jnp.int32, L)

        def per_query(q, _):
            r = wid + SC_WORKERS * q
            b = r // S
            t = r - b * S
            row0 = b * S
            pltpu.sync_copy(meta_hbm.at[r], meta_v)
            thr = meta_v[pl.ds(0, L)]
            n_eq = meta_v[pl.ds(64, L)]
            fill = jnp.zeros((L,), jnp.int32) + row0
            for j in range(n_keep // L):
                per_row = SC_MAX_INDEX_ROW // L
                idx_v[j // per_row, pl.ds((j % per_row) * L, L)] = fill

            def per_chunk(c, carry):
                pltpu.sync_copy(keys_hbm.at[r, pl.ds(c * SC_CHUNK, SC_CHUNK)], buf)

                def per_vec(j, carry):
                    cnt, eqs = carry
                    v = buf[pl.ds(j * L, L)]
                    s_vec = c * SC_CHUNK + j * L + lane
                    valid = s_vec <= t
                    gt = (v > thr) & valid
                    eq = (v == thr) & valid
                    eq_rank = plsc.cumsum(jnp.where(eq, 1, 0).astype(jnp.int32)) + eqs
                    m = gt | (eq & (eq_rank <= n_eq))
                    pos = cnt + plsc.cumsum(jnp.where(m, 1, 0).astype(jnp.int32)) - 1
                    m = m & (pos < n_keep)
                    pos = jnp.minimum(pos, n_keep - 1)
                    plsc.store_scatter(idx_v, [pos >> 7, pos & 127], s_vec + row0, mask=m)
                    cnt = cnt + plsc.all_reduce_population_count(m)
                    eqs = eqs + plsc.all_reduce_population_count(eq)
                    return cnt, eqs

                return lax.fori_loop(0, SC_CHUNK // L, per_vec, carry)

            zero = jnp.zeros((L,), jnp.int32)
            lax.fori_loop(0, t // SC_CHUNK + 1, per_chunk, (zero, zero))
            for h in range(n_idx_rows):
                pltpu.async_copy(table_hbm.at[idx_v.at[h]],
                                 rows_v.at[pl.ds(h * SC_MAX_INDEX_ROW, SC_MAX_INDEX_ROW)], sem).wait()
            pltpu.sync_copy(rows_v, rows_hbm.at[r])
            pltpu.sync_copy(idx_v, idx_hbm.at[r])
            return 0

        lax.fori_loop(0, per_worker, per_query, 0)

    return body(keys, meta, table)


def _attn_kernel(ql_ref, kv_ref, idx_ref, bias_ref, o_ref, *, S):
    tq = ql_ref.shape[0]
    n_keep = kv_ref.shape[1]
    r0 = pl.program_id(0) * tq
    row0 = (r0 // S) * S
    slot = lax.broadcasted_iota(jnp.int32, (1, n_keep), 1)
    max_exact = N_BUCKETS // 2
    def one_query(q, _):
        t = r0 + q - row0
        kv = _unpack_kv(kv_ref[q])
        ql = ql_ref[q].astype(jnp.bfloat16)
        logits = lax.dot_general(ql, kv, (((1,), (1,)), ((), ())), preferred_element_type=jnp.float32)
        ids = idx_ref[q]
        pos = jnp.concatenate([ids[j:j + 1, :] for j in range(ids.shape[0])], axis=1) - row0
        d = jnp.maximum(t - pos, 0)
        large = max_exact + (jnp.log(jnp.maximum(d, 1).astype(jnp.float32) / max_exact)
                             / math.log(MAX_DIST / max_exact) * (N_BUCKETS - max_exact)).astype(jnp.int32)
        bucket = jnp.where(d < max_exact, d, jnp.minimum(large, N_BUCKETS - 1))
        bucket = jnp.broadcast_to(bucket, (N_HEADS, n_keep))
        bias = jnp.zeros((N_HEADS, n_keep), jnp.float32)
        for bk in range(N_BUCKETS):
            bias = jnp.where(bucket == bk, bias_ref[bk], bias)
        logits = jnp.where(slot <= t, logits + bias, -1e30)
        m = jnp.max(logits, axis=-1, keepdims=True)
        e = jnp.exp(logits - m)
        p = e / jnp.sum(e, axis=-1, keepdims=True)
        o_ref[q] = jnp.dot(p.astype(jnp.bfloat16), kv, preferred_element_type=jnp.float32)
        return 0

    lax.fori_loop(0, tq, one_query, 0, unroll=2)


def _attn(qlat, rows, idx, bias_tab, S, tq=16):
    T, n_keep, W = rows.shape
    return pl.pallas_call(
        functools.partial(_attn_kernel, S=S),
        grid=(T // tq,),
        in_specs=[pl.BlockSpec((tq, N_HEADS, KV_RANK), lambda i: (i, 0, 0)),
                  pl.BlockSpec((tq, n_keep, W), lambda i: (i, 0, 0)),
                  pl.BlockSpec((tq,) + idx.shape[1:], lambda i: (i, 0, 0)),
                  pl.BlockSpec((N_BUCKETS, N_HEADS, n_keep), lambda i: (0, 0, 0))],
        out_specs=pl.BlockSpec((tq, N_HEADS, KV_RANK), lambda i: (i, 0, 0)),
        out_shape=jax.ShapeDtypeStruct((T, N_HEADS, KV_RANK), jnp.float32),
        compiler_params=_cparams("parallel"),
        name="attn",
    )(qlat, rows, idx, bias_tab)


def _oproj_kernel(o_ref, w_ref, y_ref):
    for h in range(N_HEADS):
        y_ref[:, h * HEAD_DIM:(h + 1) * HEAD_DIM] = jnp.dot(
            o_ref[:, h, :].astype(jnp.bfloat16), w_ref[h],
            preferred_element_type=jnp.float32).astype(y_ref.dtype)


def _oproj(o_lat, w_uv, tm=256):
    T = o_lat.shape[0]
    return pl.pallas_call(
        _oproj_kernel,
        grid=(T // tm,),
        in_specs=[pl.BlockSpec((tm, N_HEADS, KV_RANK), lambda i: (i, 0, 0)),
                  pl.BlockSpec((N_HEADS, KV_RANK, HEAD_DIM), lambda i: (0, 0, 0))],
        out_specs=pl.BlockSpec((tm, D_ATT), lambda i: (i, 0)),
        out_shape=jax.ShapeDtypeStruct((T, D_ATT), jnp.bfloat16),
        compiler_params=_cparams("parallel"),
        name="oproj",
    )(o_lat, w_uv)


def _merge_kernel(x_ref, yr_ref, ya_ref, gr_ref, ga_ref, pr_ref, pa_ref, wo_ref, g_ref, b_ref,
                  o_ref, *, alpha):
    a = jnp.dot(yr_ref[...], pr_ref[...], preferred_element_type=jnp.float32)
    c = jnp.dot(ya_ref[...], pa_ref[...], preferred_element_type=jnp.float32)
    merged = jax.nn.sigmoid(gr_ref[...]) * a + jax.nn.sigmoid(ga_ref[...]) * c
    mix = jnp.dot(merged.astype(jnp.bfloat16), wo_ref[...], preferred_element_type=jnp.float32)
    o_ref[...] = _layer_norm(alpha * x_ref[...] + mix, g_ref[...], b_ref[...])


def _merge(x, y_rnn, y_att, h_all, proj_rnn, proj_att, w_out, g, b, alpha, tm=512):
    T = x.shape[0]
    tok = lambda c: pl.BlockSpec((tm, D_MODEL), lambda i: (i, c))
    wgt = pl.BlockSpec((D_MODEL, D_MODEL), lambda i: (0, 0))
    vec = pl.BlockSpec((1, D_MODEL), lambda i: (0, 0))
    return pl.pallas_call(
        functools.partial(_merge_kernel, alpha=alpha),
        grid=(T // tm,),
        in_specs=[tok(0), tok(0), tok(0), tok(COL_GATE_RNN // D_MODEL), tok(COL_GATE_ATT // D_MODEL),
                  wgt, wgt, wgt, vec, vec],
        out_specs=tok(0),
        out_shape=jax.ShapeDtypeStruct((T, D_MODEL), jnp.float32),
        compiler_params=_cparams("parallel"),
        name="merge",
    )(x, y_rnn, y_att, h_all, h_all, proj_rnn, proj_att, w_out, g, b)


def _router_kernel(x_ref, w_ref, b_ref, o_ref):
    logits = lax.dot_general(w_ref[...], x_ref[...], (((1,), (1,)), ((), ())),
                             precision=lax.Precision.HIGHEST,
                             preferred_element_type=jnp.float32)
    aff = [jax.nn.sigmoid(logits[e:e + 1, :]) for e in range(N_EXPERTS)]
    sel = [aff[e] + b_ref[e:e + 1, :] for e in range(N_EXPERTS)]
    P = EXPERTS_PER_GROUP
    gscore = []
    for g in range(N_GROUPS):
        v = sel[g * P:(g + 1) * P]
        best = None
        for a in range(P):
            for c in range(a + 1, P):
                s2 = v[a] + v[c]
                best = s2 if best is None else jnp.maximum(best, s2)
        gscore.append(best)
    gbest = jnp.zeros_like(gscore[0], dtype=jnp.int32)
    gmax = gscore[0]
    for g in range(1, N_GROUPS):
        better = gscore[g] > gmax
        gbest = jnp.where(better, g, gbest)
        gmax = jnp.where(better, gscore[g], gmax)

    def pick(vals):
        out = vals[0]
        for g in range(1, N_GROUPS):
            out = jnp.where(gbest == g, vals[g], out)
        return out

    sv = [pick([sel[g * P + j] for g in range(N_GROUPS)]) for j in range(P)]
    av = [pick([aff[g * P + j] for g in range(N_GROUPS)]) for j in range(P)]

    def first_max(vals, skip=None):
        idx = None
        best = None
        for j in range(P):
            v = vals[j] if skip is None else jnp.where(skip == j, -jnp.inf, vals[j])
            if best is None:
                idx, best = jnp.zeros_like(gbest), v
            else:
                better = v > best
                idx = jnp.where(better, j, idx)
                best = jnp.where(better, v, best)
        return idx

    j1 = first_max(sv)
    j2 = first_max(sv, skip=j1)
    g1 = av[0]
    g2 = av[0]
    for j in range(1, P):
        g1 = jnp.where(j1 == j, av[j], g1)
        g2 = jnp.where(j2 == j, av[j], g2)
    den = g1 + g2
    rows = []
    for e in range(N_EXPERTS):
        g, j = divmod(e, P)
        w = jnp.where(j1 == j, g1 / den, jnp.where(j2 == j, g2 / den, 0.0))
        rows.append(jnp.where(gbest == g, w, 0.0))
    o_ref[...] = jnp.concatenate(rows, axis=0)


def _router(x, w_routerT, bias, tm=512):
    T = x.shape[0]
    return pl.pallas_call(
        _router_kernel,
        grid=(T // tm,),
        in_specs=[pl.BlockSpec((tm, D_MODEL), lambda i: (i, 0)),
                  pl.BlockSpec((N_EXPERTS, D_MODEL), lambda i: (0, 0)),
                  pl.BlockSpec((N_EXPERTS, 1), lambda i: (0, 0))],
        out_specs=pl.BlockSpec((N_EXPERTS, tm), lambda i: (0, i)),
        out_shape=jax.ShapeDtypeStruct((N_EXPERTS, T), jnp.float32),
        compiler_params=_cparams("parallel"),
        name="router",
    )(x, w_routerT, bias)


def _moe_kernel(x_ref, c_ref, wg_ref, wu_ref, wd_ref, g_ref, b_ref, o_ref, xb_ref, acc_ref, *, alpha):
    e = pl.program_id(1)

    @pl.when(e == 0)
    def _():
        xb_ref[...] = x_ref[...].astype(jnp.bfloat16)
        acc_ref[...] = jnp.zeros_like(acc_ref)

    xb = xb_ref[...]
    gate = jnp.dot(xb, wg_ref[0], preferred_element_type=jnp.float32)
    up = jnp.dot(xb, wu_ref[0], preferred_element_type=jnp.float32)
    he = (jax.nn.silu(gate) * up).astype(jnp.bfloat16)
    dn = jnp.dot(he, wd_ref[0], preferred_element_type=jnp.float32)
    lane = lax.broadcasted_iota(jnp.int32, c_ref.shape, 1)
    ce = jnp.sum(jnp.where(lane == e, c_ref[...], 0.0), axis=1, keepdims=True)
    acc_ref[...] += ce * dn

    @pl.when(e == N_EXPERTS - 1)
    def _():
        o_ref[...] = _layer_norm(alpha * x_ref[...] + acc_ref[...], g_ref[...], b_ref[...])


def _moe(x, comb, w_gate, w_up, w_down, g, b, alpha, tm=512):
    T = x.shape[0]
    vec = pl.BlockSpec((1, D_MODEL), lambda i, e: (0, 0))
    return pl.pallas_call(
        functools.partial(_moe_kernel, alpha=alpha),
        grid=(T // tm, N_EXPERTS),
        in_specs=[pl.BlockSpec((tm, D_MODEL), lambda i, e: (i, 0)),
                  pl.BlockSpec((tm, N_EXPERTS), lambda i, e: (i, 0)),
                  pl.BlockSpec((1, D_MODEL, D_EXPERT), lambda i, e: (e, 0, 0)),
                  pl.BlockSpec((1, D_MODEL, D_EXPERT), lambda i, e: (e, 0, 0)),
                  pl.BlockSpec((1, D_EXPERT, D_MODEL), lambda i, e: (e, 0, 0)),
                  vec, vec],
        out_specs=pl.BlockSpec((tm, D_MODEL), lambda i, e: (i, 0)),
        out_shape=jax.ShapeDtypeStruct((T, D_MODEL), jnp.float32),
        scratch_shapes=[pltpu.VMEM((tm, D_MODEL), jnp.bfloat16),
                        pltpu.VMEM((tm, D_MODEL), jnp.float32)],
        compiler_params=_cparams("parallel", "arbitrary"),
        name="moe",
    )(x, comb, w_gate, w_up, w_down, g, b)


def _pack_w_in(w):
    xr, gr, q, ckv, qi, ki, wi, gate_r, gate_a = jnp.split(
        w, [1024, 2048, 3072, 3328, 3840, 3904, 3912, 4936], axis=1)
    pad = jnp.zeros((w.shape[0], D_IN_PAD - w.shape[1]), w.dtype)
    return jnp.concatenate([xr, gr, q, gate_r, gate_a, qi, ckv, ki, wi, pad], axis=1).astype(jnp.bfloat16)


def _block_diag_tiles(w):
    per = 256 // RNN_BW
    w = w.reshape(D_RNN // 256, per, RNN_BW, RNN_BW)
    eye = jnp.eye(per, dtype=w.dtype)
    return jnp.einsum('cgij,gh->cgihj', w, eye).reshape(D_RNN // 256, 256, 256).astype(jnp.bfloat16)


def kernel(x, w_in, conv_w, conv_b, lru_wa, lru_ba, lru_wx, lru_bx, lru_lambda, kv_norm, w_uk, w_uv, proj_rnn, proj_att, w_out, ln1_g, ln1_b, w_router, router_bias, exp_w_gate, exp_w_up, exp_w_down, ln2_g, ln2_b, rel_bias):
    B, S, D = x.shape
    T = B * S
    depth = w_in.shape[0]
    alpha = (2 * depth) ** 0.25
    n_keep = min(TOPK_MAX, S // 4)
    bf = jnp.bfloat16
    row = lambda v: v.reshape(1, -1)

    bias_tab = jnp.broadcast_to(rel_bias.T[None].transpose(2, 1, 0), (N_BUCKETS, N_HEADS, n_keep))
    w_routerT = w_router.T
    rbias = router_bias.reshape(N_EXPERTS, 1)

    x = x.reshape(T, D)
    for l in range(depth):
        h_all = _inproj(x, _pack_w_in(w_in[l]))
        y_rnn = _rglru(h_all, B, S, conv_w[l], row(conv_b[l]), _block_diag_tiles(lru_wa[l]), row(lru_ba[l]),
                       _block_diag_tiles(lru_wx[l]), row(lru_bx[l]), row(lru_lambda[l]))
        table = _kvpack(h_all, row(kv_norm[l]))
        qlat = _qlat(h_all, jnp.swapaxes(w_uk[l], 1, 2).astype(bf))
        kidx = h_all[:, COL_KIDX:COL_KIDX + IDX_DIM].astype(bf).reshape(B, S, IDX_DIM)
        keys, meta = _indexer(h_all, kidx, B, S, n_keep)
        idx, rows = _select_gather(keys.reshape(T, S), meta.reshape(T, 128), table, B, S, n_keep)
        o_lat = _attn(qlat, rows, idx, bias_tab, S)
        y_att = _oproj(o_lat, w_uv[l].astype(bf))
        x = _merge(x, y_rnn, y_att, h_all, proj_rnn[l].astype(bf), proj_att[l].astype(bf),
                   w_out[l].astype(bf), row(ln1_g[l]), row(ln1_b[l]), alpha)
        comb = _router(x, w_routerT, rbias).T
        x = _moe(x, comb, exp_w_gate[l].astype(bf), exp_w_up[l].astype(bf), exp_w_down[l].astype(bf),
                 row(ln2_g[l]), row(ln2_b[l]), alpha)
    return x.reshape(B, S, D)
```

```python
import dataclasses
import functools
import math

import jax
import jax.numpy as jnp
from jax import lax
from jax.experimental import pallas as pl
from jax.experimental.pallas import tpu as pltpu
from jax.experimental.pallas import tpu_sc as plsc

D_MODEL = 1024
D_RNN = 1024
RNN_BLOCKS = 16
RNN_BW = D_RNN // RNN_BLOCKS
CONV_W = 4
LRU_C = 8.0
N_HEADS = 8
HEAD_DIM = 128
D_ATT = N_HEADS * HEAD_DIM
KV_RANK = 256
IDX_HEADS = 8
IDX_DIM = 64
TOPK_MAX = 256
N_BUCKETS = 32
MAX_DIST = 128
N_EXPERTS = 16
N_GROUPS = 4
EXPERTS_PER_GROUP = N_EXPERTS // N_GROUPS
D_EXPERT = 512
LN_EPS = 1e-5
RMS_EPS = 1e-6

COL_XRNN = 0
COL_GRNN = 1024
COL_Q = 2048
COL_GATE_RNN = 3072
COL_GATE_ATT = 4096
COL_QIDX = 5120
COL_CKV = 5632
COL_KIDX = 5888
D_IN_PAD = 6144

SC_CORES = 2
SC_SUBCORES = 16
SC_LANES = 16
SC_WORKERS = SC_CORES * SC_SUBCORES
SC_MAX_INDEX_ROW = 128

INT_MIN = -2 ** 31
VMEM_LIMIT = 56 * 1024 * 1024


def _cparams(*sem):
    return pltpu.CompilerParams(dimension_semantics=sem, vmem_limit_bytes=VMEM_LIMIT)


def _layer_norm(v, g, b):
    mu = jnp.mean(v, axis=-1, keepdims=True)
    var = jnp.mean(jnp.square(v - mu), axis=-1, keepdims=True)
    return (v - mu) * lax.rsqrt(var + LN_EPS) * g + b


def _inproj_kernel(x_ref, w_ref, o_ref, xb_ref):
    @pl.when(pl.program_id(1) == 0)
    def _():
        xb_ref[...] = x_ref[...].astype(jnp.bfloat16)

    o_ref[...] = jnp.dot(xb_ref[...], w_ref[...], preferred_element_type=jnp.float32)


def _inproj(x, w, tm=1024, tn=512):
    T, K = x.shape
    N = w.shape[1]
    return pl.pallas_call(
        _inproj_kernel,
        grid=(T // tm, N // tn),
        in_specs=[pl.BlockSpec((tm, K), lambda i, j: (i, 0)),
                  pl.BlockSpec((K, tn), lambda i, j: (0, j))],
        out_specs=pl.BlockSpec((tm, tn), lambda i, j: (i, j)),
        out_shape=jax.ShapeDtypeStruct((T, N), jnp.float32),
        scratch_shapes=[pltpu.VMEM((tm, K), jnp.bfloat16)],
        compiler_params=_cparams("parallel", "arbitrary"),
        name="inproj",
    )(x, w)


def _rglru_kernel(x_ref, g_ref, cw_ref, cb_ref, wa_ref, ba_ref, wx_ref, bx_ref, lam_ref,
                  o_ref, prev_ref, carry_ref, a_ref, u_ref):
    ts = x_ref.shape[0]

    @pl.when(pl.program_id(1) == 0)
    def _():
        prev_ref[...] = jnp.zeros_like(prev_ref)
        carry_ref[...] = jnp.zeros_like(carry_ref)

    x = x_ref[...]
    xe = jnp.concatenate([prev_ref[...], x], axis=0)
    xr = cb_ref[...] + sum(cw_ref[k:k + 1, :] * xe[5 + k:5 + k + ts, :] for k in range(CONV_W))
    prev_ref[...] = x[ts - 8:, :]

    xb = xr.astype(jnp.bfloat16)
    nt = D_RNN // 256
    ra = jnp.concatenate([jnp.dot(xb[:, c * 256:(c + 1) * 256], wa_ref[c],
                                  preferred_element_type=jnp.float32) for c in range(nt)], axis=1)
    rx = jnp.concatenate([jnp.dot(xb[:, c * 256:(c + 1) * 256], wx_ref[c],
                                  preferred_element_type=jnp.float32) for c in range(nt)], axis=1)
    r = jax.nn.sigmoid(ra + ba_ref[...])
    gi = jax.nn.sigmoid(rx + bx_ref[...])
    z = -lam_ref[...]
    softplus = jnp.maximum(z, 0.0) + jnp.log(1.0 + jnp.exp(-jnp.abs(z)))
    log_a = (-LRU_C * r) * softplus
    a_ref[...] = jnp.exp(log_a)
    u_ref[...] = jnp.sqrt(1.0 - jnp.exp(2.0 * log_a)) * (gi * xr)

    row = lax.broadcasted_iota(jnp.int32, (8, D_RNN), 0)

    def group(gidx, carry):
        r0 = pl.multiple_of(gidx * 8, 8)
        a8 = a_ref[pl.ds(r0, 8), :]
        u8 = u_ref[pl.ds(r0, 8), :]
        for d in (1, 2, 4):
            keep = row >= d
            a_sh = pltpu.roll(a8, d, 0)
            u_sh = pltpu.roll(u8, d, 0)
            u8 = jnp.where(keep, a8 * u_sh + u8, u8)
            a8 = jnp.where(keep, a8 * a_sh, a8)
        h8 = a8 * carry + u8
        u_ref[pl.ds(r0, 8), :] = h8
        return h8[7:8, :]

    carry_ref[...] = lax.fori_loop(0, ts // 8, group, carry_ref[...], unroll=4)
    o_ref[...] = (u_ref[...] * jax.nn.gelu(g_ref[...])).astype(o_ref.dtype)


def _rglru(h_all, B, S, cw, cb, wa, ba, wx, bx, lam, ts=256):
    nblk = S // ts
    row = lambda b, i: b * nblk + i
    vec = pl.BlockSpec((1, D_RNN), lambda b, i: (0, 0))
    tile = pl.BlockSpec((D_RNN // 256, 256, 256), lambda b, i: (0, 0, 0))
    return pl.pallas_call(
        _rglru_kernel,
        grid=(B, nblk),
        in_specs=[pl.BlockSpec((ts, D_RNN), lambda b, i: (row(b, i), COL_XRNN // D_RNN)),
                  pl.BlockSpec((ts, D_RNN), lambda b, i: (row(b, i), COL_GRNN // D_RNN)),
                  pl.BlockSpec((CONV_W, D_RNN), lambda b, i: (0, 0)),
                  vec, tile, vec, tile, vec, vec],
        out_specs=pl.BlockSpec((ts, D_RNN), lambda b, i: (row(b, i), 0)),
        out_shape=jax.ShapeDtypeStruct((B * S, D_RNN), jnp.bfloat16),
        scratch_shapes=[pltpu.VMEM((8, D_RNN), jnp.float32),
                        pltpu.VMEM((1, D_RNN), jnp.float32),
                        pltpu.VMEM((ts, D_RNN), jnp.float32),
                        pltpu.VMEM((ts, D_RNN), jnp.float32)],
        compiler_params=_cparams("arbitrary", "arbitrary"),
        name="rglru",
    )(h_all, h_all, cw, cb, wa, ba, wx, bx, lam)


def _kvpack_kernel(c_ref, g_ref, o_ref):
    c = c_ref[...]
    cn = c * lax.rsqrt(jnp.mean(jnp.square(c), axis=-1, keepdims=True) + RMS_EPS) * g_ref[...]
    cb = cn.astype(jnp.bfloat16).astype(jnp.float32)
    half = KV_RANK // 2
    lo = lax.bitcast_convert_type(cb[:, :half], jnp.int32)
    hi = lax.bitcast_convert_type(cb[:, half:], jnp.int32)
    o_ref[...] = (hi & jnp.int32(-65536)) | lax.shift_right_logical(lo, 16)


def _kvpack(h_all, kv_norm, ts=1024):
    T = h_all.shape[0]
    return pl.pallas_call(
        _kvpack_kernel,
        grid=(T // ts,),
        in_specs=[pl.BlockSpec((ts, KV_RANK), lambda i: (i, COL_CKV // KV_RANK)),
                  pl.BlockSpec((1, KV_RANK), lambda i: (0, 0))],
        out_specs=pl.BlockSpec((ts, KV_RANK // 2), lambda i: (i, 0)),
        out_shape=jax.ShapeDtypeStruct((T, KV_RANK // 2), jnp.int32),
        compiler_params=_cparams("parallel"),
        name="kvpack",
    )(h_all, kv_norm)


def _unpack_kv(w):
    lo = lax.bitcast_convert_type(lax.shift_left(w, 16), jnp.float32)
    hi = lax.bitcast_convert_type(w & jnp.int32(-65536), jnp.float32)
    return jnp.concatenate([lo, hi], axis=-1).astype(jnp.bfloat16)


def _qlat_kernel(q_ref, w_ref, o_ref):
    q = q_ref[...].astype(jnp.bfloat16)
    for h in range(N_HEADS):
        o_ref[:, h, :] = jnp.dot(q[:, h * HEAD_DIM:(h + 1) * HEAD_DIM], w_ref[h],
                                 preferred_element_type=jnp.float32) * (HEAD_DIM ** -0.5)


def _qlat(h_all, w_ukT, tm=256):
    T = h_all.shape[0]
    return pl.pallas_call(
        _qlat_kernel,
        grid=(T // tm,),
        in_specs=[pl.BlockSpec((tm, D_ATT), lambda i: (i, COL_Q // D_ATT)),
                  pl.BlockSpec((N_HEADS, HEAD_DIM, KV_RANK), lambda i: (0, 0, 0))],
        out_specs=pl.BlockSpec((tm, N_HEADS, KV_RANK), lambda i: (i, 0, 0)),
        out_shape=jax.ShapeDtypeStruct((T, N_HEADS, KV_RANK), jnp.float32),
        compiler_params=_cparams("parallel"),
        name="qlat",
    )(h_all, w_ukT)


IDX_QB = 128
IDX_KC = 512


def _indexer_kernel(q_ref, kw_ref, k_ref, keys_ref, thr_ref, *, n_keep):
    S = keys_ref.shape[2]
    i = pl.program_id(1)
    n_causal = (i * IDX_QB + IDX_QB - 1) // IDX_KC + 1
    q_all = jnp.concatenate(
        [(q_ref[:, h * IDX_DIM:(h + 1) * IDX_DIM] * (IDX_DIM ** -0.5)).astype(jnp.bfloat16)
         for h in range(IDX_HEADS)], axis=0)
    ws = [kw_ref[:, IDX_DIM + h:IDX_DIM + h + 1] * (IDX_HEADS ** -0.5) for h in range(IDX_HEADS)]
    t = i * IDX_QB + lax.broadcasted_iota(jnp.int32, (IDX_QB, IDX_KC), 0)
    lane = lax.broadcasted_iota(jnp.int32, (IDX_QB, IDX_KC), 1)
    n_slab = IDX_KC // 128

    def score_chunk(c, carry):
        top1, top2 = carry
        c0 = pl.multiple_of(c * IDX_KC, IDX_KC)
        kc = k_ref[0, pl.ds(c0, IDX_KC), :]
        d = lax.dot_general(q_all, kc, (((1,), (1,)), ((), ())), preferred_element_type=jnp.float32)
        sc = jnp.zeros((IDX_QB, IDX_KC), jnp.float32)
        for h in range(IDX_HEADS):
            sc = sc + jnp.maximum(d[h * IDX_QB:(h + 1) * IDX_QB], 0.0) * ws[h]
        bits = lax.bitcast_convert_type(sc, jnp.int32)
        key = bits ^ (lax.shift_right_arithmetic(bits, 31) & jnp.int32(0x7FFFFFFF))
        key = jnp.where(c0 + lane <= t, key, jnp.int32(INT_MIN))
        keys_ref[0, :, pl.ds(c0, IDX_KC)] = key
        for j in range(n_slab):
            x = key[:, j * 128:(j + 1) * 128]
            top2 = jnp.maximum(top2, jnp.minimum(top1, x))
            top1 = jnp.maximum(top1, x)
        return top1, top2

    floor = jnp.full((IDX_QB, 128), INT_MIN, jnp.int32)
    top1, top2 = lax.fori_loop(0, n_causal, score_chunk, (floor, floor))

    def blank_chunk(c, _):
        c0 = pl.multiple_of(c * IDX_KC, IDX_KC)
        keys_ref[0, :, pl.ds(c0, IDX_KC)] = jnp.full((IDX_QB, IDX_KC), INT_MIN, jnp.int32)
        return 0

    lax.fori_loop(n_causal, S // IDX_KC, blank_chunk, 0)

    def count(pred):
        def chunk(c, acc):
            c0 = pl.multiple_of(c * IDX_KC, IDX_KC)
            hit = jnp.where(pred(keys_ref[0, :, pl.ds(c0, IDX_KC)], c0), 1, 0)
            for j in range(n_slab):
                acc = acc + hit[:, j * 128:(j + 1) * 128]
            return acc

        acc = lax.fori_loop(0, n_causal, chunk, jnp.zeros((IDX_QB, 128), jnp.int32))
        return jnp.sum(acc, axis=1, keepdims=True)

    def wide(v):
        return jnp.broadcast_to(v, (IDX_QB, IDX_KC))

    def any_row(flag):
        return jnp.max(jnp.where(flag, 1, 0))

    t_col = t[:, :1]
    short = t_col + 1 < n_keep
    lo0 = jnp.min(top2, axis=1, keepdims=True)
    hi0 = jnp.max(top1, axis=1, keepdims=True) + 1
    unknown = jnp.int32(S + 1)
    cnt0 = jnp.where(short, n_keep, unknown)

    def active_rows(lo, hi, cnt):
        return (cnt != n_keep) & (lo + 1 < hi)

    def search_cond(carry):
        return carry[3] > 0

    def search_body(carry):
        lo, hi, cnt, _ = carry
        act = active_rows(lo, hi, cnt)
        mid = (lo & hi) + lax.shift_right_arithmetic(lo ^ hi, 1)
        mid_b = wide(mid)
        n = count(lambda kk, c0: kk >= mid_b)
        up = act & (n >= n_keep)
        down = act & (n < n_keep)
        lo = jnp.where(up, mid, lo)
        cnt = jnp.where(up, n, cnt)
        hi = jnp.where(down, mid, hi)
        return lo, hi, cnt, any_row(active_rows(lo, hi, cnt))

    lo, hi, cnt, _ = lax.while_loop(search_cond, search_body,
                                    (lo0, hi0, cnt0, any_row(active_rows(lo0, hi0, cnt0))))
    tie = cnt != n_keep

    @pl.when(any_row(tie) > 0)
    def _():
        lo_b = wide(lo)
        need = n_keep - count(lambda kk, c0: kk > lo_b)

        def pos_bit(p, res):
            cand = res | lax.shift_left(jnp.int32(1), (S - 1).bit_length() - 1 - p)
            cand_b = wide(cand)
            n = count(lambda kk, c0: (kk == lo_b) & (c0 + lane < cand_b))
            return jnp.where(n < need, cand, res)

        cut_b = wide(lax.fori_loop(0, (S - 1).bit_length(), pos_bit, jnp.zeros((IDX_QB, 1), jnp.int32)))
        tie_b = wide(tie)

        def bump(c, _):
            c0 = pl.multiple_of(c * IDX_KC, IDX_KC)
            kk = keys_ref[0, :, pl.ds(c0, IDX_KC)]
            hit = tie_b & (kk == lo_b) & (c0 + lane <= cut_b)
            keys_ref[0, :, pl.ds(c0, IDX_KC)] = jnp.where(hit, kk + 1, kk)
            return 0

        lax.fori_loop(0, n_causal, bump, 0)

    thr = jnp.where(short, jnp.int32(INT_MIN), jnp.where(tie, lo, lo - 1))
    thr_ref[0] = jnp.broadcast_to(thr, (IDX_QB, 128))


def _indexer(h_all, kidx, B, S, n_keep):
    nblk = S // IDX_QB
    row = lambda b, i: b * nblk + i
    return pl.pallas_call(
        functools.partial(_indexer_kernel, n_keep=n_keep),
        grid=(B, nblk),
        in_specs=[pl.BlockSpec((IDX_QB, IDX_HEADS * IDX_DIM), lambda b, i: (row(b, i), COL_QIDX // 512)),
                  pl.BlockSpec((IDX_QB, 128), lambda b, i: (row(b, i), COL_KIDX // 128)),
                  pl.BlockSpec((1, S, IDX_DIM), lambda b, i: (b, 0, 0))],
        out_specs=[pl.BlockSpec((1, IDX_QB, S), lambda b, i: (b, i, 0)),
                   pl.BlockSpec((1, IDX_QB, 128), lambda b, i: (b, i, 0))],
        out_shape=[jax.ShapeDtypeStruct((B, S, S), jnp.int32),
                   jax.ShapeDtypeStruct((B, S, 128), jnp.int32)],
        compiler_params=_cparams("parallel", "arbitrary"),
        name="indexer",
    )(h_all, h_all, kidx)


SC_SEG = 4096


def _select_gather(keys, thr_w, table, B, S, n_keep):
    T = B * S
    W = table.shape[1]
    L = SC_LANES
    per_worker = T // SC_WORKERS
    n_idx_rows = n_keep // SC_MAX_INDEX_ROW
    mesh = plsc.VectorSubcoreMesh(core_axis_name="c", subcore_axis_name="s")
    cp = pltpu.CompilerParams()
    if "needs_layout_passes" in pltpu.CompilerParams.__dataclass_fields__:
        cp = dataclasses.replace(cp, needs_layout_passes=False)

    seg = min(SC_SEG, S)
    n_seg = S // seg
    unroll = 8
    per_row = SC_MAX_INDEX_ROW // L

    @functools.partial(
        pl.kernel, mesh=mesh, compiler_params=cp,
        out_type=(jax.ShapeDtypeStruct((T, n_idx_rows, SC_MAX_INDEX_ROW), jnp.int32),
                  jax.ShapeDtypeStruct((T, n_keep, W), jnp.int32)),
        scratch_types=[pltpu.VMEM((2, S), jnp.int32),
                       pltpu.VMEM((per_worker,), jnp.int32),
                       pltpu.VMEM((n_idx_rows, SC_MAX_INDEX_ROW), jnp.int32),
                       pltpu.VMEM((2, n_idx_rows, SC_MAX_INDEX_ROW), jnp.int32),
                       pltpu.VMEM((2, n_keep, W), jnp.int32),
                       pltpu.SemaphoreType.DMA((2,)),
                       pltpu.SemaphoreType.DMA((2,)),
                       pltpu.SemaphoreType.DMA((2,))],
        name="select_gather",
    )
    def body(keys_hbm, thr_hbm, table_hbm, idx_hbm, rows_hbm,
             krow, thr_v, idx_s, idx_g, rows_v, key_sem, gat_sem, out_sem):
        wid = lax.axis_index("s") * SC_CORES + lax.axis_index("c")
        lane = lax.iota(jnp.int32, L)
        zero = jnp.zeros((L,), jnp.int32)

        def key_copy(r, k, buf):
            return pltpu.make_async_copy(keys_hbm.at[r, pl.ds(k * seg, seg)],
                                         krow.at[buf, pl.ds(k * seg, seg)], key_sem.at[buf])

        def gather_copy(h, buf):
            return pltpu.make_async_copy(
                table_hbm.at[idx_g.at[buf, h]],
                rows_v.at[buf, pl.ds(h * SC_MAX_INDEX_ROW, SC_MAX_INDEX_ROW)], gat_sem.at[buf])

        def out_copies(r, buf):
            return (pltpu.make_async_copy(rows_v.at[buf], rows_hbm.at[r], out_sem.at[buf]),
                    pltpu.make_async_copy(idx_g.at[buf], idx_hbm.at[r], out_sem.at[buf]))

        def row_of(q):
            r = wid + SC_WORKERS * q
            b = r // S
            return r, r - b * S, b * S

        def fetch_keys(q, buf):
            r, t, _ = row_of(q)
            for k in range(n_seg):
                @pl.when((q < per_worker) & (t >= k * seg))
                def _():
                    key_copy(r, k, buf).start()

        def step(q, buf):
            r, t, row0 = row_of(q)
            fetch_keys(q + 1, 1 - buf)
            for k in range(n_seg):
                @pl.when(t >= k * seg)
                def _():
                    key_copy(r, k, buf).wait()
            thr = plsc.load_gather(thr_v, [zero + q])
            fill = zero + row0
            for j in range(n_keep // L):
                idx_s[j // per_row, pl.ds((j % per_row) * L, L)] = fill

            def scan(g, cnt):
                for u in range(unroll):
                    j = g * unroll + u
                    v = krow[buf, pl.ds(j * L, L)]
                    m = v > thr
                    pos = cnt + plsc.cumsum(jnp.where(m, 1, 0).astype(jnp.int32)) - 1
                    m = m & (pos < n_keep)
                    pos = jnp.minimum(pos, n_keep - 1)
                    plsc.store_scatter(idx_s, [pos >> 7, pos & 127], j * L + lane + row0, mask=m)
                    cnt = cnt + plsc.all_reduce_population_count(m)
                return cnt

            lax.fori_loop(0, t // (unroll * L) + 1, scan, zero)

            @pl.when(q >= 2)
            def _():
                for c in out_copies(r, buf):
                    c.wait()

            for j in range(n_keep // L):
                sl = (j // per_row, pl.ds((j % per_row) * L, L))
                idx_g[(buf,) + sl] = idx_s[sl]

            @pl.when(q >= 1)
            def _():
                for h in range(n_idx_rows):
                    gather_copy(h, 1 - buf).wait()
                for c in out_copies(r - SC_WORKERS, 1 - buf):
                    c.start()

            for h in range(n_idx_rows):
                gather_copy(h, buf).start()

        pltpu.sync_copy(thr_hbm.at[wid], thr_v)
        fetch_keys(jnp.int32(0), 0)

        def pair(p, _):
            step(2 * p, 0)
            step(2 * p + 1, 1)
            return 0

        lax.fori_loop(0, per_worker // 2, pair, 0)

        r_last = row_of(per_worker - 1)[0]
        for h in range(n_idx_rows):
            gather_copy(h, 1).wait()
        for c in out_copies(r_last, 1):
            c.start()
        for c in out_copies(r_last, 0) + out_copies(r_last, 1):
            c.wait()

    return body(keys, thr_w, table)


def _attn_kernel(ql_ref, kv_ref, idx_ref, tbl_ref, o_ref, kvb_ref, lg_ref, bk_ref, *, S):
    tq = ql_ref.shape[0]
    n_keep = kv_ref.shape[1]
    r0 = pl.program_id(0) * tq
    row0 = (r0 // S) * S
    t0 = r0 - row0
    max_exact = N_BUCKETS // 2

    t_row = t0 + lax.broadcasted_iota(jnp.int32, (tq, n_keep), 0)
    d = jnp.maximum(t_row - (idx_ref[...] - row0), 0)
    large = max_exact + (jnp.log(jnp.maximum(d, 1).astype(jnp.float32) / max_exact)
                         / math.log(MAX_DIST / max_exact) * (N_BUCKETS - max_exact)).astype(jnp.int32)
    bk_ref[...] = jnp.where(d < max_exact, d, jnp.minimum(large, N_BUCKETS - 1))
    bucket_id = lax.broadcasted_iota(jnp.int32, (N_BUCKETS, n_keep), 0)

    def logits_of(q, _):
        kv = _unpack_kv(kv_ref[q])
        kvb_ref[q] = kv
        onehot = jnp.where(bk_ref[pl.ds(q, 1), :] == bucket_id, 1.0, 0.0).astype(jnp.bfloat16)
        bias2 = jnp.dot(tbl_ref[...], onehot, preferred_element_type=jnp.float32)
        ql = ql_ref[q].astype(jnp.bfloat16)
        lg_ref[q] = (lax.dot_general(ql, kv, (((1,), (1,)), ((), ())), preferred_element_type=jnp.float32)
                     + bias2[:N_HEADS] + bias2[N_HEADS:])
        return 0

    lax.fori_loop(0, tq, logits_of, 0, unroll=4)

    shape = (tq, N_HEADS, n_keep)
    valid = lax.broadcasted_iota(jnp.int32, shape, 2) <= t0 + lax.broadcasted_iota(jnp.int32, shape, 0)
    logits = jnp.where(valid, lg_ref[...], -1e30)
    e = jnp.exp(logits - jnp.max(logits, axis=-1, keepdims=True))
    lg_ref[...] = e / jnp.sum(e, axis=-1, keepdims=True)

    def values_of(q, _):
        o_ref[q] = jnp.dot(lg_ref[q].astype(jnp.bfloat16), kvb_ref[q], preferred_element_type=jnp.float32)
        return 0

    lax.fori_loop(0, tq, values_of, 0, unroll=4)


def _attn(qlat, rows, idx, bias_tab, S, tq=32):
    T, n_keep, W = rows.shape
    return pl.pallas_call(
        functools.partial(_attn_kernel, S=S),
        grid=(T // tq,),
        in_specs=[pl.BlockSpec((tq, N_HEADS, KV_RANK), lambda i: (i, 0, 0)),
                  pl.BlockSpec((tq, n_keep, W), lambda i: (i, 0, 0)),
                  pl.BlockSpec((tq, n_keep), lambda i: (i, 0)),
                  pl.BlockSpec((2 * N_HEADS, N_BUCKETS), lambda i: (0, 0))],
        out_specs=pl.BlockSpec((tq, N_HEADS, KV_RANK), lambda i: (i, 0, 0)),
        out_shape=jax.ShapeDtypeStruct((T, N_HEADS, KV_RANK), jnp.float32),
        scratch_shapes=[pltpu.VMEM((tq, n_keep, KV_RANK), jnp.bfloat16),
                        pltpu.VMEM((tq, N_HEADS, n_keep), jnp.float32),
                        pltpu.VMEM((tq, n_keep), jnp.int32)],
        compiler_params=_cparams("parallel"),
        name="attn",
    )(qlat, rows, idx, bias_tab)


def _oproj_kernel(o_ref, w_ref, y_ref):
    for h in range(N_HEADS):
        y_ref[:, h * HEAD_DIM:(h + 1) * HEAD_DIM] = jnp.dot(
            o_ref[:, h, :].astype(jnp.bfloat16), w_ref[h],
            preferred_element_type=jnp.float32).astype(y_ref.dtype)


def _oproj(o_lat, w_uv, tm=256):
    T = o_lat.shape[0]
    return pl.pallas_call(
        _oproj_kernel,
        grid=(T // tm,),
        in_specs=[pl.BlockSpec((tm, N_HEADS, KV_RANK), lambda i: (i, 0, 0)),
                  pl.BlockSpec((N_HEADS, KV_RANK, HEAD_DIM), lambda i: (0, 0, 0))],
        out_specs=pl.BlockSpec((tm, D_ATT), lambda i: (i, 0)),
        out_shape=jax.ShapeDtypeStruct((T, D_ATT), jnp.bfloat16),
        compiler_params=_cparams("parallel"),
        name="oproj",
    )(o_lat, w_uv)


def _merge_kernel(x_ref, yr_ref, ya_ref, gr_ref, ga_ref, pr_ref, pa_ref, wo_ref, g_ref, b_ref,
                  o_ref, *, alpha):
    a = jnp.dot(yr_ref[...], pr_ref[...], preferred_element_type=jnp.float32)
    c = jnp.dot(ya_ref[...], pa_ref[...], preferred_element_type=jnp.float32)
    merged = jax.nn.sigmoid(gr_ref[...]) * a + jax.nn.sigmoid(ga_ref[...]) * c
    mix = jnp.dot(merged.astype(jnp.bfloat16), wo_ref[...], preferred_element_type=jnp.float32)
    o_ref[...] = _layer_norm(alpha * x_ref[...] + mix, g_ref[...], b_ref[...])


def _merge(x, y_rnn, y_att, h_all, proj_rnn, proj_att, w_out, g, b, alpha, tm=512):
    T = x.shape[0]
    tok = lambda c: pl.BlockSpec((tm, D_MODEL), lambda i: (i, c))
    wgt = pl.BlockSpec((D_MODEL, D_MODEL), lambda i: (0, 0))
    vec = pl.BlockSpec((1, D_MODEL), lambda i: (0, 0))
    return pl.pallas_call(
        functools.partial(_merge_kernel, alpha=alpha),
        grid=(T // tm,),
        in_specs=[tok(0), tok(0), tok(0), tok(COL_GATE_RNN // D_MODEL), tok(COL_GATE_ATT // D_MODEL),
                  wgt, wgt, wgt, vec, vec],
        out_specs=tok(0),
        out_shape=jax.ShapeDtypeStruct((T, D_MODEL), jnp.float32),
        compiler_params=_cparams("parallel"),
        name="merge",
    )(x, y_rnn, y_att, h_all, h_all, proj_rnn, proj_att, w_out, g, b)


def _router_kernel(x_ref, w_ref, b_ref, o_ref):
    logits = lax.dot_general(w_ref[...], x_ref[...], (((1,), (1,)), ((), ())),
                             precision=lax.Precision.HIGHEST,
                             preferred_element_type=jnp.float32)
    aff = [jax.nn.sigmoid(logits[e:e + 1, :]) for e in range(N_EXPERTS)]
    sel = [aff[e] + b_ref[e:e + 1, :] for e in range(N_EXPERTS)]
    P = EXPERTS_PER_GROUP
    gscore = []
    for g in range(N_GROUPS):
        v = sel[g * P:(g + 1) * P]
        best = None
        for a in range(P):
            for c in range(a + 1, P):
                s2 = v[a] + v[c]
                best = s2 if best is None else jnp.maximum(best, s2)
        gscore.append(best)
    gbest = jnp.zeros_like(gscore[0], dtype=jnp.int32)
    gmax = gscore[0]
    for g in range(1, N_GROUPS):
        better = gscore[g] > gmax
        gbest = jnp.where(better, g, gbest)
        gmax = jnp.where(better, gscore[g], gmax)

    def pick(vals):
        out = vals[0]
        for g in range(1, N_GROUPS):
            out = jnp.where(gbest == g, vals[g], out)
        return out

    sv = [pick([sel[g * P + j] for g in range(N_GROUPS)]) for j in range(P)]
    av = [pick([aff[g * P + j] for g in range(N_GROUPS)]) for j in range(P)]

    def first_max(vals, skip=None):
        idx = None
        best = None
        for j in range(P):
            v = vals[j] if skip is None else jnp.where(skip == j, -jnp.inf, vals[j])
            if best is None:
                idx, best = jnp.zeros_like(gbest), v
            else:
                better = v > best
                idx = jnp.where(better, j, idx)
                best = jnp.where(better, v, best)
        return idx

    j1 = first_max(sv)
    j2 = first_max(sv, skip=j1)
    g1 = av[0]
    g2 = av[0]
    for j in range(1, P):
        g1 = jnp.where(j1 == j, av[j], g1)
        g2 = jnp.where(j2 == j, av[j], g2)
    den = g1 + g2
    rows = []
    for e in range(N_EXPERTS):
        g, j = divmod(e, P)
        w = jnp.where(j1 == j, g1 / den, jnp.where(j2 == j, g2 / den, 0.0))
        rows.append(jnp.where(gbest == g, w, 0.0))
    o_ref[...] = jnp.concatenate(rows, axis=0)


def _router(x, w_routerT, bias, tm=512):
    T = x.shape[0]
    return pl.pallas_call(
        _router_kernel,
        grid=(T // tm,),
        in_specs=[pl.BlockSpec((tm, D_MODEL), lambda i: (i, 0)),
                  pl.BlockSpec((N_EXPERTS, D_MODEL), lambda i: (0, 0)),
                  pl.BlockSpec((N_EXPERTS, 1), lambda i: (0, 0))],
        out_specs=pl.BlockSpec((N_EXPERTS, tm), lambda i: (0, i)),
        out_shape=jax.ShapeDtypeStruct((N_EXPERTS, T), jnp.float32),
        compiler_params=_cparams("parallel"),
        name="router",
    )(x, w_routerT, bias)


def _moe_kernel(x_ref, c_ref, wg_ref, wu_ref, wd_ref, g_ref, b_ref, o_ref, xb_ref, acc_ref, *, alpha):
    e = pl.program_id(1)

    @pl.when(e == 0)
    def _():
        xb_ref[...] = x_ref[...].astype(jnp.bfloat16)
        acc_ref[...] = jnp.zeros_like(acc_ref)

    xb = xb_ref[...]
    gate = jnp.dot(xb, wg_ref[0], preferred_element_type=jnp.float32)
    up = jnp.dot(xb, wu_ref[0], preferred_element_type=jnp.float32)
    he = (jax.nn.silu(gate) * up).astype(jnp.bfloat16)
    dn = jnp.dot(he, wd_ref[0], preferred_element_type=jnp.float32)
    lane = lax.broadcasted_iota(jnp.int32, c_ref.shape, 1)
    ce = jnp.sum(jnp.where(lane == e, c_ref[...], 0.0), axis=1, keepdims=True)
    acc_ref[...] += ce * dn

    @pl.when(e == N_EXPERTS - 1)
    def _():
        o_ref[...] = _layer_norm(alpha * x_ref[...] + acc_ref[...], g_ref[...], b_ref[...])


def _moe(x, comb, w_gate, w_up, w_down, g, b, alpha, tm=512):
    T = x.shape[0]
    vec = pl.BlockSpec((1, D_MODEL), lambda i, e: (0, 0))
    return pl.pallas_call(
        functools.partial(_moe_kernel, alpha=alpha),
        grid=(T // tm, N_EXPERTS),
        in_specs=[pl.BlockSpec((tm, D_MODEL), lambda i, e: (i, 0)),
                  pl.BlockSpec((tm, N_EXPERTS), lambda i, e: (i, 0)),
                  pl.BlockSpec((1, D_MODEL, D_EXPERT), lambda i, e: (e, 0, 0)),
                  pl.BlockSpec((1, D_MODEL, D_EXPERT), lambda i, e: (e, 0, 0)),
                  pl.BlockSpec((1, D_EXPERT, D_MODEL), lambda i, e: (e, 0, 0)),
                  vec, vec],
        out_specs=pl.BlockSpec((tm, D_MODEL), lambda i, e: (i, 0)),
        out_shape=jax.ShapeDtypeStruct((T, D_MODEL), jnp.float32),
        scratch_shapes=[pltpu.VMEM((tm, D_MODEL), jnp.bfloat16),
                        pltpu.VMEM((tm, D_MODEL), jnp.float32)],
        compiler_params=_cparams("parallel", "arbitrary"),
        name="moe",
    )(x, comb, w_gate, w_up, w_down, g, b)


def _pack_w_in(w):
    xr, gr, q, ckv, qi, ki, wi, gate_r, gate_a = jnp.split(
        w, [1024, 2048, 3072, 3328, 3840, 3904, 3912, 4936], axis=1)
    pad = jnp.zeros((w.shape[0], D_IN_PAD - w.shape[1]), w.dtype)
    return jnp.concatenate([xr, gr, q, gate_r, gate_a, qi, ckv, ki, wi, pad], axis=1).astype(jnp.bfloat16)


def _block_diag_tiles(w):
    per = 256 // RNN_BW
    w = w.reshape(D_RNN // 256, per, RNN_BW, RNN_BW)
    eye = jnp.eye(per, dtype=w.dtype)
    return jnp.einsum('cgij,gh->cgihj', w, eye).reshape(D_RNN // 256, 256, 256).astype(jnp.bfloat16)


def kernel(x, w_in, conv_w, conv_b, lru_wa, lru_ba, lru_wx, lru_bx, lru_lambda, kv_norm, w_uk, w_uv, proj_rnn, proj_att, w_out, ln1_g, ln1_b, w_router, router_bias, exp_w_gate, exp_w_up, exp_w_down, ln2_g, ln2_b, rel_bias):
    B, S, D = x.shape
    T = B * S
    depth = w_in.shape[0]
    alpha = (2 * depth) ** 0.25
    n_keep = min(TOPK_MAX, S // 4)
    bf = jnp.bfloat16
    row = lambda v: v.reshape(1, -1)

    bias_hi = rel_bias.T.astype(bf)
    bias_lo = (rel_bias.T - bias_hi.astype(jnp.float32)).astype(bf)
    bias_tab = jnp.concatenate([bias_hi, bias_lo], axis=0)
    w_routerT = w_router.T
    rbias = router_bias.reshape(N_EXPERTS, 1)

    x = x.reshape(T, D)
    for l in range(depth):
        h_all = _inproj(x, _pack_w_in(w_in[l]))
        y_rnn = _rglru(h_all, B, S, conv_w[l], row(conv_b[l]), _block_diag_tiles(lru_wa[l]), row(lru_ba[l]),
                       _block_diag_tiles(lru_wx[l]), row(lru_bx[l]), row(lru_lambda[l]))
        table = _kvpack(h_all, row(kv_norm[l]))
        qlat = _qlat(h_all, jnp.swapaxes(w_uk[l], 1, 2).astype(bf))
        kidx = h_all[:, COL_KIDX:COL_KIDX + IDX_DIM].astype(bf).reshape(B, S, IDX_DIM)
        keys, thr = _indexer(h_all, kidx, B, S, n_keep)
        thr_w = thr[:, :, 0].reshape(T // SC_WORKERS, SC_WORKERS).T
        idx, rows = _select_gather(keys.reshape(T, S), thr_w, table, B, S, n_keep)
        o_lat = _attn(qlat, rows, idx.reshape(T, n_keep), bias_tab, S)
        y_att = _oproj(o_lat, w_uv[l].astype(bf))
        x = _merge(x, y_rnn, y_att, h_all, proj_rnn[l].astype(bf), proj_att[l].astype(bf),
                   w_out[l].astype(bf), row(ln1_g[l]), row(ln1_b[l]), alpha)
        comb = _router(x, w_routerT, rbias).T
        x = _moe(x, comb, exp_w_gate[l].astype(bf), exp_w_up[l].astype(bf), exp_w_down[l].astype(bf),
                 row(ln2_g[l]), row(ln2_b[l]), alpha)
    return x.reshape(B, S, D)
```

```python
import dataclasses
import functools
import math

import jax
import jax.numpy as jnp
from jax import lax
from jax.experimental import pallas as pl
from jax.experimental.pallas import tpu as pltpu
from jax.experimental.pallas import tpu_sc as plsc

D_MODEL = 1024
D_RNN = 1024
RNN_BLOCKS = 16
RNN_BW = D_RNN // RNN_BLOCKS
CONV_W = 4
LRU_C = 8.0
N_HEADS = 8
HEAD_DIM = 128
D_ATT = N_HEADS * HEAD_DIM
KV_RANK = 256
IDX_HEADS = 8
IDX_DIM = 64
TOPK_MAX = 256
N_BUCKETS = 32
MAX_DIST = 128
N_EXPERTS = 16
N_GROUPS = 4
EXPERTS_PER_GROUP = N_EXPERTS // N_GROUPS
D_EXPERT = 512
LN_EPS = 1e-5
RMS_EPS = 1e-6

COL_XRNN = 0
COL_GRNN = 1024
COL_Q = 2048
COL_GATE_RNN = 3072
COL_GATE_ATT = 4096
COL_QIDX = 5120
COL_CKV = 5632
COL_KIDX = 5888
D_IN_PAD = 6144

SC_CORES = 2
SC_SUBCORES = 16
SC_LANES = 16
SC_WORKERS = SC_CORES * SC_SUBCORES
SC_MAX_INDEX_ROW = 128

INT_MIN = -2 ** 31
VMEM_LIMIT = 56 * 1024 * 1024


def _cparams(*sem):
    return pltpu.CompilerParams(dimension_semantics=sem, vmem_limit_bytes=VMEM_LIMIT)


def _layer_norm(v, g, b):
    mu = jnp.mean(v, axis=-1, keepdims=True)
    var = jnp.mean(jnp.square(v - mu), axis=-1, keepdims=True)
    return (v - mu) * lax.rsqrt(var + LN_EPS) * g + b


def _inproj_kernel(x_ref, w_ref, o_ref, xb_ref):
    @pl.when(pl.program_id(1) == 0)
    def _():
        xb_ref[...] = x_ref[...].astype(jnp.bfloat16)

    o_ref[...] = jnp.dot(xb_ref[...], w_ref[...], preferred_element_type=jnp.float32)


def _inproj(x, w, tm=1024, tn=512):
    T, K = x.shape
    N = w.shape[1]
    return pl.pallas_call(
        _inproj_kernel,
        grid=(T // tm, N // tn),
        in_specs=[pl.BlockSpec((tm, K), lambda i, j: (i, 0)),
                  pl.BlockSpec((K, tn), lambda i, j: (0, j))],
        out_specs=pl.BlockSpec((tm, tn), lambda i, j: (i, j)),
        out_shape=jax.ShapeDtypeStruct((T, N), jnp.float32),
        scratch_shapes=[pltpu.VMEM((tm, K), jnp.bfloat16)],
        compiler_params=_cparams("parallel", "arbitrary"),
        name="inproj",
    )(x, w)


def _rglru_kernel(x_ref, g_ref, cw_ref, cb_ref, wa_ref, ba_ref, wx_ref, bx_ref, lam_ref,
                  o_ref, prev_ref, carry_ref, a_ref, u_ref):
    ts = x_ref.shape[0]

    @pl.when(pl.program_id(1) == 0)
    def _():
        prev_ref[...] = jnp.zeros_like(prev_ref)
        carry_ref[...] = jnp.zeros_like(carry_ref)

    x = x_ref[...]
    xe = jnp.concatenate([prev_ref[...], x], axis=0)
    xr = cb_ref[...] + sum(cw_ref[k:k + 1, :] * xe[5 + k:5 + k + ts, :] for k in range(CONV_W))
    prev_ref[...] = x[ts - 8:, :]

    xb = xr.astype(jnp.bfloat16)
    nt = D_RNN // 256
    ra = jnp.concatenate([jnp.dot(xb[:, c * 256:(c + 1) * 256], wa_ref[c],
                                  preferred_element_type=jnp.float32) for c in range(nt)], axis=1)
    rx = jnp.concatenate([jnp.dot(xb[:, c * 256:(c + 1) * 256], wx_ref[c],
                                  preferred_element_type=jnp.float32) for c in range(nt)], axis=1)
    r = jax.nn.sigmoid(ra + ba_ref[...])
    gi = jax.nn.sigmoid(rx + bx_ref[...])
    z = -lam_ref[...]
    softplus = jnp.maximum(z, 0.0) + jnp.log(1.0 + jnp.exp(-jnp.abs(z)))
    log_a = (-LRU_C * r) * softplus
    a_ref[...] = jnp.exp(log_a)
    u_ref[...] = jnp.sqrt(1.0 - jnp.exp(2.0 * log_a)) * (gi * xr)

    row = lax.broadcasted_iota(jnp.int32, (8, D_RNN), 0)

    def group(gidx, carry):
        r0 = pl.multiple_of(gidx * 8, 8)
        a8 = a_ref[pl.ds(r0, 8), :]
        u8 = u_ref[pl.ds(r0, 8), :]
        for d in (1, 2, 4):
            keep = row >= d
            a_sh = pltpu.roll(a8, d, 0)
            u_sh = pltpu.roll(u8, d, 0)
            u8 = jnp.where(keep, a8 * u_sh + u8, u8)
            a8 = jnp.where(keep, a8 * a_sh, a8)
        h8 = a8 * carry + u8
        u_ref[pl.ds(r0, 8), :] = h8
        return h8[7:8, :]

    carry_ref[...] = lax.fori_loop(0, ts // 8, group, carry_ref[...], unroll=4)
    o_ref[...] = (u_ref[...] * jax.nn.gelu(g_ref[...])).astype(o_ref.dtype)


def _rglru(h_all, B, S, cw, cb, wa, ba, wx, bx, lam, ts=256):
    nblk = S // ts
    row = lambda b, i: b * nblk + i
    vec = pl.BlockSpec((1, D_RNN), lambda b, i: (0, 0))
    tile = pl.BlockSpec((D_RNN // 256, 256, 256), lambda b, i: (0, 0, 0))
    return pl.pallas_call(
        _rglru_kernel,
        grid=(B, nblk),
        in_specs=[pl.BlockSpec((ts, D_RNN), lambda b, i: (row(b, i), COL_XRNN // D_RNN)),
                  pl.BlockSpec((ts, D_RNN), lambda b, i: (row(b, i), COL_GRNN // D_RNN)),
                  pl.BlockSpec((CONV_W, D_RNN), lambda b, i: (0, 0)),
                  vec, tile, vec, tile, vec, vec],
        out_specs=pl.BlockSpec((ts, D_RNN), lambda b, i: (row(b, i), 0)),
        out_shape=jax.ShapeDtypeStruct((B * S, D_RNN), jnp.bfloat16),
        scratch_shapes=[pltpu.VMEM((8, D_RNN), jnp.float32),
                        pltpu.VMEM((1, D_RNN), jnp.float32),
                        pltpu.VMEM((ts, D_RNN), jnp.float32),
                        pltpu.VMEM((ts, D_RNN), jnp.float32)],
        compiler_params=_cparams("arbitrary", "arbitrary"),
        name="rglru",
    )(h_all, h_all, cw, cb, wa, ba, wx, bx, lam)


def _kvpack_kernel(c_ref, g_ref, o_ref):
    c = c_ref[...]
    cn = c * lax.rsqrt(jnp.mean(jnp.square(c), axis=-1, keepdims=True) + RMS_EPS) * g_ref[...]
    cb = cn.astype(jnp.bfloat16).astype(jnp.float32)
    half = KV_RANK // 2
    lo = lax.bitcast_convert_type(cb[:, :half], jnp.int32)
    hi = lax.bitcast_convert_type(cb[:, half:], jnp.int32)
    o_ref[...] = (hi & jnp.int32(-65536)) | lax.shift_right_logical(lo, 16)


def _kvpack(h_all, kv_norm, ts=1024):
    T = h_all.shape[0]
    return pl.pallas_call(
        _kvpack_kernel,
        grid=(T // ts,),
        in_specs=[pl.BlockSpec((ts, KV_RANK), lambda i: (i, COL_CKV // KV_RANK)),
                  pl.BlockSpec((1, KV_RANK), lambda i: (0, 0))],
        out_specs=pl.BlockSpec((ts, KV_RANK // 2), lambda i: (i, 0)),
        out_shape=jax.ShapeDtypeStruct((T, KV_RANK // 2), jnp.int32),
        compiler_params=_cparams("parallel"),
        name="kvpack",
    )(h_all, kv_norm)


def _unpack_kv(w):
    lo = lax.bitcast_convert_type(lax.shift_left(w, 16), jnp.float32)
    hi = lax.bitcast_convert_type(w & jnp.int32(-65536), jnp.float32)
    return jnp.concatenate([lo, hi], axis=-1).astype(jnp.bfloat16)


def _qlat_kernel(q_ref, w_ref, o_ref):
    q = q_ref[...].astype(jnp.bfloat16)
    for h in range(N_HEADS):
        o_ref[:, h, :] = jnp.dot(q[:, h * HEAD_DIM:(h + 1) * HEAD_DIM], w_ref[h],
                                 preferred_element_type=jnp.float32) * (HEAD_DIM ** -0.5)


def _qlat(h_all, w_ukT, tm=256):
    T = h_all.shape[0]
    return pl.pallas_call(
        _qlat_kernel,
        grid=(T // tm,),
        in_specs=[pl.BlockSpec((tm, D_ATT), lambda i: (i, COL_Q // D_ATT)),
                  pl.BlockSpec((N_HEADS, HEAD_DIM, KV_RANK), lambda i: (0, 0, 0))],
        out_specs=pl.BlockSpec((tm, N_HEADS, KV_RANK), lambda i: (i, 0, 0)),
        out_shape=jax.ShapeDtypeStruct((T, N_HEADS, KV_RANK), jnp.float32),
        compiler_params=_cparams("parallel"),
        name="qlat",
    )(h_all, w_ukT)


IDX_QB = 128
IDX_KC = 512
IDX_VALUE_STEPS = 40


def _indexer_kernel(q_ref, kw_ref, k_ref, keys_ref, thr_ref, *, n_keep):
    S = keys_ref.shape[2]
    i = pl.program_id(0)
    n_causal = (i * IDX_QB + IDX_QB - 1) // IDX_KC + 1
    q_all = jnp.concatenate(
        [(q_ref[:, h * IDX_DIM:(h + 1) * IDX_DIM] * (IDX_DIM ** -0.5)).astype(jnp.bfloat16)
         for h in range(IDX_HEADS)], axis=0)
    ws = [kw_ref[:, IDX_DIM + h:IDX_DIM + h + 1] * (IDX_HEADS ** -0.5) for h in range(IDX_HEADS)]
    t = i * IDX_QB + lax.broadcasted_iota(jnp.int32, (IDX_QB, IDX_KC), 0)
    lane = lax.broadcasted_iota(jnp.int32, (IDX_QB, IDX_KC), 1)
    n_slab = IDX_KC // 128

    def score_chunk(c, carry):
        top1, top2 = carry
        c0 = pl.multiple_of(c * IDX_KC, IDX_KC)
        kc = k_ref[0, pl.ds(c0, IDX_KC), :]
        d = lax.dot_general(q_all, kc, (((1,), (1,)), ((), ())), preferred_element_type=jnp.float32)
        sc = jnp.zeros((IDX_QB, IDX_KC), jnp.float32)
        for h in range(IDX_HEADS):
            sc = sc + jnp.maximum(d[h * IDX_QB:(h + 1) * IDX_QB], 0.0) * ws[h]
        bits = lax.bitcast_convert_type(sc, jnp.int32)
        key = bits ^ (lax.shift_right_arithmetic(bits, 31) & jnp.int32(0x7FFFFFFF))
        key = jnp.where(c0 + lane <= t, key, jnp.int32(INT_MIN))
        keys_ref[0, :, pl.ds(c0, IDX_KC)] = key
        for j in range(n_slab):
            x = key[:, j * 128:(j + 1) * 128]
            top2 = jnp.maximum(top2, jnp.minimum(top1, x))
            top1 = jnp.maximum(top1, x)
        return top1, top2

    floor = jnp.full((IDX_QB, 128), INT_MIN, jnp.int32)
    top1, top2 = lax.fori_loop(0, n_causal, score_chunk, (floor, floor))

    def blank_chunk(c, _):
        c0 = pl.multiple_of(c * IDX_KC, IDX_KC)
        keys_ref[0, :, pl.ds(c0, IDX_KC)] = jnp.full((IDX_QB, IDX_KC), INT_MIN, jnp.int32)
        return 0

    lax.fori_loop(n_causal, S // IDX_KC, blank_chunk, 0)

    def count(pred):
        def chunk(c, acc):
            c0 = pl.multiple_of(c * IDX_KC, IDX_KC)
            hit = jnp.where(pred(keys_ref[0, :, pl.ds(c0, IDX_KC)], c0), 1, 0)
            for j in range(n_slab):
                acc = acc + hit[:, j * 128:(j + 1) * 128]
            return acc

        acc = lax.fori_loop(0, n_causal, chunk, jnp.zeros((IDX_QB, 128), jnp.int32))
        return jnp.sum(acc, axis=1, keepdims=True)

    def wide(v):
        return jnp.broadcast_to(v, (IDX_QB, IDX_KC))

    def any_row(flag):
        return jnp.max(jnp.where(flag, 1, 0))

    t_col = t[:, :1]
    short = t_col + 1 < n_keep
    lo0 = jnp.min(top2, axis=1, keepdims=True)
    hi0 = jnp.max(top1, axis=1, keepdims=True) + 1
    unknown = jnp.int32(S + 1)
    cnt0 = jnp.where(short, n_keep, unknown)

    def active_rows(lo, hi, cnt):
        return (cnt != n_keep) & (lo + 1 < hi)

    def search_cond(carry):
        return carry[3] > 0

    def flip(v):
        return v ^ (lax.shift_right_arithmetic(v, 31) & jnp.int32(0x7FFFFFFF))

    def search_body(carry):
        lo, hi, cnt, _, step = carry
        act = active_rows(lo, hi, cnt)
        mid = (lo & hi) + lax.shift_right_arithmetic(lo ^ hi, 1)
        mean = (0.5 * lax.bitcast_convert_type(flip(lo), jnp.float32)
                + 0.5 * lax.bitcast_convert_type(flip(hi), jnp.float32))
        mid_v = flip(lax.bitcast_convert_type(mean, jnp.int32))
        width = hi - lo
        far = (width < 0) | (width > (1 << 24))
        mid = jnp.where(far & (step < IDX_VALUE_STEPS) & (mid_v > lo) & (mid_v < hi), mid_v, mid)
        mid = jnp.where((lo < 0) & (hi > 0), 0, mid)
        mid = jnp.where((lo == 0) & (hi > 1), 1, mid)
        mid_b = wide(mid)
        n = count(lambda kk, c0: kk >= mid_b)
        up = act & (n >= n_keep)
        down = act & (n < n_keep)
        lo = jnp.where(up, mid, lo)
        cnt = jnp.where(up, n, cnt)
        hi = jnp.where(down, mid, hi)
        return lo, hi, cnt, any_row(active_rows(lo, hi, cnt)), step + 1

    lo, hi, cnt, _, _ = lax.while_loop(
        search_cond, search_body,
        (lo0, hi0, cnt0, any_row(active_rows(lo0, hi0, cnt0)), jnp.int32(0)))
    tie = cnt != n_keep
    lo_b = wide(lo)
    n_gt = count(lambda kk, c0: kk > lo_b)
    thr = jnp.where(short, jnp.int32(INT_MIN), jnp.where(tie, lo, lo - 1))
    n_eq = jnp.where(tie, n_keep - n_gt, 0)
    left = lax.broadcasted_iota(jnp.int32, (IDX_QB, 128), 1) < 64
    thr_ref[0] = jnp.where(left, jnp.broadcast_to(thr, (IDX_QB, 128)), jnp.broadcast_to(n_eq, (IDX_QB, 128)))


def _indexer(h_all, kidx, b, S, n_keep):
    nblk = S // IDX_QB
    return pl.pallas_call(
        functools.partial(_indexer_kernel, n_keep=n_keep),
        grid=(nblk,),
        in_specs=[pl.BlockSpec((IDX_QB, IDX_HEADS * IDX_DIM), lambda i: (b * nblk + i, COL_QIDX // 512)),
                  pl.BlockSpec((IDX_QB, 128), lambda i: (b * nblk + i, COL_KIDX // 128)),
                  pl.BlockSpec((1, S, IDX_DIM), lambda i: (b, 0, 0))],
        out_specs=[pl.BlockSpec((1, IDX_QB, S), lambda i: (0, i, 0)),
                   pl.BlockSpec((1, IDX_QB, 128), lambda i: (0, i, 0))],
        out_shape=[jax.ShapeDtypeStruct((1, S, S), jnp.int32),
                   jax.ShapeDtypeStruct((1, S, 128), jnp.int32)],
        compiler_params=_cparams("parallel"),
        name="indexer",
    )(h_all, h_all, kidx)


SC_SEG = 4096


def _select_gather(keys, thr_w, neq_w, table, base, n_keep):
    T, S = keys.shape
    W = table.shape[1]
    L = SC_LANES
    per_worker = T // SC_WORKERS
    n_idx_rows = n_keep // SC_MAX_INDEX_ROW
    mesh = plsc.VectorSubcoreMesh(core_axis_name="c", subcore_axis_name="s")
    cp = pltpu.CompilerParams()
    if "needs_layout_passes" in pltpu.CompilerParams.__dataclass_fields__:
        cp = dataclasses.replace(cp, needs_layout_passes=False)

    seg = min(SC_SEG, S)
    n_seg = S // seg
    unroll = 8
    per_row = SC_MAX_INDEX_ROW // L

    @functools.partial(
        pl.kernel, mesh=mesh, compiler_params=cp,
        out_type=(jax.ShapeDtypeStruct((T, n_idx_rows, SC_MAX_INDEX_ROW), jnp.int32),
                  jax.ShapeDtypeStruct((T, n_keep, W), jnp.int32)),
        scratch_types=[pltpu.VMEM((2, S), jnp.int32),
                       pltpu.VMEM((per_worker,), jnp.int32),
                       pltpu.VMEM((per_worker,), jnp.int32),
                       pltpu.VMEM((n_idx_rows, SC_MAX_INDEX_ROW), jnp.int32),
                       pltpu.VMEM((2, n_idx_rows, SC_MAX_INDEX_ROW), jnp.int32),
                       pltpu.VMEM((2, n_keep, W), jnp.int32),
                       pltpu.SemaphoreType.DMA((2,)),
                       pltpu.SemaphoreType.DMA((2,)),
                       pltpu.SemaphoreType.DMA((2,))],
        name="select_gather",
    )
    def body(keys_hbm, thr_hbm, neq_hbm, table_hbm, idx_hbm, rows_hbm,
             krow, thr_v, neq_v, idx_s, idx_g, rows_v, key_sem, gat_sem, out_sem):
        wid = lax.axis_index("s") * SC_CORES + lax.axis_index("c")
        lane = lax.iota(jnp.int32, L)
        zero = jnp.zeros((L,), jnp.int32)

        def key_copy(r, k, buf):
            return pltpu.make_async_copy(keys_hbm.at[r, pl.ds(k * seg, seg)],
                                         krow.at[buf, pl.ds(k * seg, seg)], key_sem.at[buf])

        def gather_copy(h, buf):
            return pltpu.make_async_copy(
                table_hbm.at[idx_g.at[buf, h]],
                rows_v.at[buf, pl.ds(h * SC_MAX_INDEX_ROW, SC_MAX_INDEX_ROW)], gat_sem.at[buf])

        def out_copies(r, buf):
            return (pltpu.make_async_copy(rows_v.at[buf], rows_hbm.at[r], out_sem.at[buf]),
                    pltpu.make_async_copy(idx_g.at[buf], idx_hbm.at[r], out_sem.at[buf]))

        def row_of(q):
            r = wid + SC_WORKERS * q
            return r, r, base

        def fetch_keys(q, buf):
            r, t, _ = row_of(q)
            for k in range(n_seg):
                @pl.when((q < per_worker) & (t >= k * seg))
                def _():
                    key_copy(r, k, buf).start()

        def step(q, buf):
            r, t, row0 = row_of(q)
            fetch_keys(q + 1, 1 - buf)
            for k in range(n_seg):
                @pl.when(t >= k * seg)
                def _():
                    key_copy(r, k, buf).wait()
            thr = plsc.load_gather(thr_v, [zero + q])
            n_eq = plsc.load_gather(neq_v, [zero + q])
            fill = zero + row0
            for j in range(n_keep // L):
                idx_s[j // per_row, pl.ds((j % per_row) * L, L)] = fill

            def emit(j, m, cnt):
                pos = cnt + plsc.cumsum(jnp.where(m, 1, 0).astype(jnp.int32)) - 1
                ok = m & (pos < n_keep)
                pos = jnp.minimum(pos, n_keep - 1)
                plsc.store_scatter(idx_s, [pos >> 7, pos & 127], j * L + lane + row0, mask=ok)
                return cnt + plsc.all_reduce_population_count(m)

            def scan(g, cnt):
                for u in range(unroll):
                    j = g * unroll + u
                    cnt = emit(j, krow[buf, pl.ds(j * L, L)] > thr, cnt)
                return cnt

            def scan_ties(g, carry):
                cnt, seen = carry
                for u in range(unroll):
                    j = g * unroll + u
                    v = krow[buf, pl.ds(j * L, L)]
                    eq = v == thr
                    rank = seen + plsc.cumsum(jnp.where(eq, 1, 0).astype(jnp.int32))
                    cnt = emit(j, (v > thr) | (eq & (rank <= n_eq)), cnt)
                    seen = seen + plsc.all_reduce_population_count(eq)
                return cnt, seen

            trips = t // (unroll * L) + 1

            def with_ties():
                lax.fori_loop(0, trips, scan_ties, (zero, zero))

            def without_ties():
                lax.fori_loop(0, trips, scan, zero)

            lax.cond(jnp.max(n_eq) > 0, with_ties, without_ties)

            @pl.when(q >= 2)
            def _():
                for c in out_copies(r, buf):
                    c.wait()

            for j in range(n_keep // L):
                sl = (j // per_row, pl.ds((j % per_row) * L, L))
                idx_g[(buf,) + sl] = idx_s[sl]

            @pl.when(q >= 1)
            def _():
                for h in range(n_idx_rows):
                    gather_copy(h, 1 - buf).wait()
                for c in out_copies(r - SC_WORKERS, 1 - buf):
                    c.start()

            for h in range(n_idx_rows):
                gather_copy(h, buf).start()

        pltpu.sync_copy(thr_hbm.at[wid], thr_v)
        pltpu.sync_copy(neq_hbm.at[wid], neq_v)
        fetch_keys(jnp.int32(0), 0)

        def pair(p, _):
            step(2 * p, 0)
            step(2 * p + 1, 1)
            return 0

        lax.fori_loop(0, per_worker // 2, pair, 0)

        r_last = row_of(per_worker - 1)[0]
        for h in range(n_idx_rows):
            gather_copy(h, 1).wait()
        for c in out_copies(r_last, 1):
            c.start()
        for c in out_copies(r_last, 0) + out_copies(r_last, 1):
            c.wait()

    return body(keys, thr_w, neq_w, table)


def _attn_kernel(ql_ref, kv_ref, idx_ref, tbl_ref, o_ref, kvb_ref, lg_ref, bk_ref, *, base):
    tq = ql_ref.shape[0]
    n_keep = kv_ref.shape[1]
    t0 = pl.program_id(0) * tq
    row0 = base
    max_exact = N_BUCKETS // 2

    t_row = t0 + lax.broadcasted_iota(jnp.int32, (tq, n_keep), 0)
    d = jnp.maximum(t_row - (idx_ref[...] - row0), 0)
    large = max_exact + (jnp.log(jnp.maximum(d, 1).astype(jnp.float32) / max_exact)
                         / math.log(MAX_DIST / max_exact) * (N_BUCKETS - max_exact)).astype(jnp.int32)
    bk_ref[...] = jnp.where(d < max_exact, d, jnp.minimum(large, N_BUCKETS - 1))
    bucket_id = lax.broadcasted_iota(jnp.int32, (N_BUCKETS, n_keep), 0)

    def logits_of(q, _):
        kv = _unpack_kv(kv_ref[q])
        kvb_ref[q] = kv
        onehot = jnp.where(bk_ref[pl.ds(q, 1), :] == bucket_id, 1.0, 0.0).astype(jnp.bfloat16)
        bias2 = jnp.dot(tbl_ref[...], onehot, preferred_element_type=jnp.float32)
        ql = ql_ref[q].astype(jnp.bfloat16)
        lg_ref[q] = (lax.dot_general(ql, kv, (((1,), (1,)), ((), ())), preferred_element_type=jnp.float32)
                     + bias2[:N_HEADS] + bias2[N_HEADS:])
        return 0

    lax.fori_loop(0, tq, logits_of, 0, unroll=4)

    shape = (tq, N_HEADS, n_keep)
    valid = lax.broadcasted_iota(jnp.int32, shape, 2) <= t0 + lax.broadcasted_iota(jnp.int32, shape, 0)
    logits = jnp.where(valid, lg_ref[...], -1e30)
    e = jnp.exp(logits - jnp.max(logits, axis=-1, keepdims=True))
    lg_ref[...] = e / jnp.sum(e, axis=-1, keepdims=True)

    def values_of(q, _):
        o_ref[q] = jnp.dot(lg_ref[q].astype(jnp.bfloat16), kvb_ref[q], preferred_element_type=jnp.float32)
        return 0

    lax.fori_loop(0, tq, values_of, 0, unroll=4)


def _attn(qlat, rows, idx, bias_tab, b, tq=32):
    T, n_keep, W = rows.shape
    nblk = T // tq
    return pl.pallas_call(
        functools.partial(_attn_kernel, base=b * T),
        grid=(nblk,),
        in_specs=[pl.BlockSpec((tq, N_HEADS, KV_RANK), lambda i: (b * nblk + i, 0, 0)),
                  pl.BlockSpec((tq, n_keep, W), lambda i: (i, 0, 0)),
                  pl.BlockSpec((tq, n_keep), lambda i: (i, 0)),
                  pl.BlockSpec((2 * N_HEADS, N_BUCKETS), lambda i: (0, 0))],
        out_specs=pl.BlockSpec((tq, N_HEADS, KV_RANK), lambda i: (i, 0, 0)),
        out_shape=jax.ShapeDtypeStruct((T, N_HEADS, KV_RANK), jnp.float32),
        scratch_shapes=[pltpu.VMEM((tq, n_keep, KV_RANK), jnp.bfloat16),
                        pltpu.VMEM((tq, N_HEADS, n_keep), jnp.float32),
                        pltpu.VMEM((tq, n_keep), jnp.int32)],
        compiler_params=_cparams("parallel"),
        name="attn",
    )(qlat, rows, idx, bias_tab)


def _oproj_kernel(o_ref, w_ref, y_ref):
    for h in range(N_HEADS):
        y_ref[:, h * HEAD_DIM:(h + 1) * HEAD_DIM] = jnp.dot(
            o_ref[:, h, :].astype(jnp.bfloat16), w_ref[h],
            preferred_element_type=jnp.float32).astype(y_ref.dtype)


def _oproj(o_lat, w_uv, tm=256):
    T = o_lat.shape[0]
    return pl.pallas_call(
        _oproj_kernel,
        grid=(T // tm,),
        in_specs=[pl.BlockSpec((tm, N_HEADS, KV_RANK), lambda i: (i, 0, 0)),
                  pl.BlockSpec((N_HEADS, KV_RANK, HEAD_DIM), lambda i: (0, 0, 0))],
        out_specs=pl.BlockSpec((tm, D_ATT), lambda i: (i, 0)),
        out_shape=jax.ShapeDtypeStruct((T, D_ATT), jnp.bfloat16),
        compiler_params=_cparams("parallel"),
        name="oproj",
    )(o_lat, w_uv)


def _merge_kernel(x_ref, yr_ref, ya_ref, gr_ref, ga_ref, pr_ref, pa_ref, wo_ref, g_ref, b_ref,
                  o_ref, *, alpha):
    a = jnp.dot(yr_ref[...], pr_ref[...], preferred_element_type=jnp.float32)
    c = jnp.dot(ya_ref[...], pa_ref[...], preferred_element_type=jnp.float32)
    merged = jax.nn.sigmoid(gr_ref[...]) * a + jax.nn.sigmoid(ga_ref[...]) * c
    mix = jnp.dot(merged.astype(jnp.bfloat16), wo_ref[...], preferred_element_type=jnp.float32)
    o_ref[...] = _layer_norm(alpha * x_ref[...] + mix, g_ref[...], b_ref[...])


def _merge(x, y_rnn, y_att, h_all, proj_rnn, proj_att, w_out, g, b, alpha, tm=512):
    T = x.shape[0]
    tok = lambda c: pl.BlockSpec((tm, D_MODEL), lambda i: (i, c))
    wgt = pl.BlockSpec((D_MODEL, D_MODEL), lambda i: (0, 0))
    vec = pl.BlockSpec((1, D_MODEL), lambda i: (0, 0))
    return pl.pallas_call(
        functools.partial(_merge_kernel, alpha=alpha),
        grid=(T // tm,),
        in_specs=[tok(0), tok(0), tok(0), tok(COL_GATE_RNN // D_MODEL), tok(COL_GATE_ATT // D_MODEL),
                  wgt, wgt, wgt, vec, vec],
        out_specs=tok(0),
        out_shape=jax.ShapeDtypeStruct((T, D_MODEL), jnp.float32),
        compiler_params=_cparams("parallel"),
        name="merge",
    )(x, y_rnn, y_att, h_all, h_all, proj_rnn, proj_att, w_out, g, b)


def _router_kernel(x_ref, w_ref, b_ref, o_ref):
    logits = lax.dot_general(w_ref[...], x_ref[...], (((1,), (1,)), ((), ())),
                             precision=lax.Precision.HIGHEST,
                             preferred_element_type=jnp.float32)
    aff = [jax.nn.sigmoid(logits[e:e + 1, :]) for e in range(N_EXPERTS)]
    sel = [aff[e] + b_ref[e:e + 1, :] for e in range(N_EXPERTS)]
    P = EXPERTS_PER_GROUP
    gscore = []
    for g in range(N_GROUPS):
        v = sel[g * P:(g + 1) * P]
        best = None
        for a in range(P):
            for c in range(a + 1, P):
                s2 = v[a] + v[c]
                best = s2 if best is None else jnp.maximum(best, s2)
        gscore.append(best)
    gbest = jnp.zeros_like(gscore[0], dtype=jnp.int32)
    gmax = gscore[0]
    for g in range(1, N_GROUPS):
        better = gscore[g] > gmax
        gbest = jnp.where(better, g, gbest)
        gmax = jnp.where(better, gscore[g], gmax)

    def pick(vals):
        out = vals[0]
        for g in range(1, N_GROUPS):
            out = jnp.where(gbest == g, vals[g], out)
        return out

    sv = [pick([sel[g * P + j] for g in range(N_GROUPS)]) for j in range(P)]
    av = [pick([aff[g * P + j] for g in range(N_GROUPS)]) for j in range(P)]

    def first_max(vals, skip=None):
        idx = None
        best = None
        for j in range(P):
            v = vals[j] if skip is None else jnp.where(skip == j, -jnp.inf, vals[j])
            if best is None:
                idx, best = jnp.zeros_like(gbest), v
            else:
                better = v > best
                idx = jnp.where(better, j, idx)
                best = jnp.where(better, v, best)
        return idx

    j1 = first_max(sv)
    j2 = first_max(sv, skip=j1)
    g1 = av[0]
    g2 = av[0]
    for j in range(1, P):
        g1 = jnp.where(j1 == j, av[j], g1)
        g2 = jnp.where(j2 == j, av[j], g2)
    den = g1 + g2
    rows = []
    for e in range(N_EXPERTS):
        g, j = divmod(e, P)
        w = jnp.where(j1 == j, g1 / den, jnp.where(j2 == j, g2 / den, 0.0))
        rows.append(jnp.where(gbest == g, w, 0.0))
    o_ref[...] = jnp.concatenate(rows, axis=0)


def _router(x, w_routerT, bias, tm=512):
    T = x.shape[0]
    return pl.pallas_call(
        _router_kernel,
        grid=(T // tm,),
        in_specs=[pl.BlockSpec((tm, D_MODEL), lambda i: (i, 0)),
                  pl.BlockSpec((N_EXPERTS, D_MODEL), lambda i: (0, 0)),
                  pl.BlockSpec((N_EXPERTS, 1), lambda i: (0, 0))],
        out_specs=pl.BlockSpec((N_EXPERTS, tm), lambda i: (0, i)),
        out_shape=jax.ShapeDtypeStruct((N_EXPERTS, T), jnp.float32),
        compiler_params=_cparams("parallel"),
        name="router",
    )(x, w_routerT, bias)


def _moe_kernel(x_ref, c_ref, wg_ref, wu_ref, wd_ref, g_ref, b_ref, o_ref, xb_ref, acc_ref, *, alpha):
    e = pl.program_id(1)

    @pl.when(e == 0)
    def _():
        xb_ref[...] = x_ref[...].astype(jnp.bfloat16)
        acc_ref[...] = jnp.zeros_like(acc_ref)

    xb = xb_ref[...]
    gate = jnp.dot(xb, wg_ref[0], preferred_element_type=jnp.float32)
    up = jnp.dot(xb, wu_ref[0], preferred_element_type=jnp.float32)
    he = (jax.nn.silu(gate) * up).astype(jnp.bfloat16)
    dn = jnp.dot(he, wd_ref[0], preferred_element_type=jnp.float32)
    lane = lax.broadcasted_iota(jnp.int32, c_ref.shape, 1)
    ce = jnp.sum(jnp.where(lane == e, c_ref[...], 0.0), axis=1, keepdims=True)
    acc_ref[...] += ce * dn

    @pl.when(e == N_EXPERTS - 1)
    def _():
        o_ref[...] = _layer_norm(alpha * x_ref[...] + acc_ref[...], g_ref[...], b_ref[...])


def _moe(x, comb, w_gate, w_up, w_down, g, b, alpha, tm=512):
    T = x.shape[0]
    vec = pl.BlockSpec((1, D_MODEL), lambda i, e: (0, 0))
    return pl.pallas_call(
        functools.partial(_moe_kernel, alpha=alpha),
        grid=(T // tm, N_EXPERTS),
        in_specs=[pl.BlockSpec((tm, D_MODEL), lambda i, e: (i, 0)),
                  pl.BlockSpec((tm, N_EXPERTS), lambda i, e: (i, 0)),
                  pl.BlockSpec((1, D_MODEL, D_EXPERT), lambda i, e: (e, 0, 0)),
                  pl.BlockSpec((1, D_MODEL, D_EXPERT), lambda i, e: (e, 0, 0)),
                  pl.BlockSpec((1, D_EXPERT, D_MODEL), lambda i, e: (e, 0, 0)),
                  vec, vec],
        out_specs=pl.BlockSpec((tm, D_MODEL), lambda i, e: (i, 0)),
        out_shape=jax.ShapeDtypeStruct((T, D_MODEL), jnp.float32),
        scratch_shapes=[pltpu.VMEM((tm, D_MODEL), jnp.bfloat16),
                        pltpu.VMEM((tm, D_MODEL), jnp.float32)],
        compiler_params=_cparams("parallel", "arbitrary"),
        name="moe",
    )(x, comb, w_gate, w_up, w_down, g, b)


def _pack_w_in(w):
    xr, gr, q, ckv, qi, ki, wi, gate_r, gate_a = jnp.split(
        w, [1024, 2048, 3072, 3328, 3840, 3904, 3912, 4936], axis=1)
    pad = jnp.zeros((w.shape[0], D_IN_PAD - w.shape[1]), w.dtype)
    return jnp.concatenate([xr, gr, q, gate_r, gate_a, qi, ckv, ki, wi, pad], axis=1).astype(jnp.bfloat16)


def _block_diag_tiles(w):
    per = 256 // RNN_BW
    w = w.reshape(D_RNN // 256, per, RNN_BW, RNN_BW)
    eye = jnp.eye(per, dtype=w.dtype)
    return jnp.einsum('cgij,gh->cgihj', w, eye).reshape(D_RNN // 256, 256, 256).astype(jnp.bfloat16)


def kernel(x, w_in, conv_w, conv_b, lru_wa, lru_ba, lru_wx, lru_bx, lru_lambda, kv_norm, w_uk, w_uv, proj_rnn, proj_att, w_out, ln1_g, ln1_b, w_router, router_bias, exp_w_gate, exp_w_up, exp_w_down, ln2_g, ln2_b, rel_bias):
    B, S, D = x.shape
    T = B * S
    depth = w_in.shape[0]
    alpha = (2 * depth) ** 0.25
    n_keep = min(TOPK_MAX, S // 4)
    bf = jnp.bfloat16
    row = lambda v: v.reshape(1, -1)

    bias_hi = rel_bias.T.astype(bf)
    bias_lo = (rel_bias.T - bias_hi.astype(jnp.float32)).astype(bf)
    bias_tab = jnp.concatenate([bias_hi, bias_lo], axis=0)
    w_routerT = w_router.T
    rbias = router_bias.reshape(N_EXPERTS, 1)

    x = x.reshape(T, D)
    for l in range(depth):
        h_all = _inproj(x, _pack_w_in(w_in[l]))
        y_rnn = _rglru(h_all, B, S, conv_w[l], row(conv_b[l]), _block_diag_tiles(lru_wa[l]), row(lru_ba[l]),
                       _block_diag_tiles(lru_wx[l]), row(lru_bx[l]), row(lru_lambda[l]))
        table = _kvpack(h_all, row(kv_norm[l]))
        qlat = _qlat(h_all, jnp.swapaxes(w_uk[l], 1, 2).astype(bf))
        kidx = h_all[:, COL_KIDX:COL_KIDX + IDX_DIM].astype(bf).reshape(B, S, IDX_DIM)
        gathered = []
        for b in range(B):
            keys, meta = _indexer(h_all, kidx, b, S, n_keep)
            per_worker = lambda col: meta[0, :, col].reshape(S // SC_WORKERS, SC_WORKERS).T
            gathered.append(_select_gather(keys[0], per_worker(0), per_worker(64), table, b * S, n_keep))
        w_uv_l = w_uv[l].astype(bf)
        y_att = jnp.concatenate(
            [_oproj(_attn(qlat, rows, idx.reshape(S, n_keep), bias_tab, b), w_uv_l)
             for b, (idx, rows) in enumerate(gathered)], axis=0)
        x = _merge(x, y_rnn, y_att, h_all, proj_rnn[l].astype(bf), proj_att[l].astype(bf),
                   w_out[l].astype(bf), row(ln1_g[l]), row(ln1_b[l]), alpha)
        comb = _router(x, w_routerT, rbias).T
        x = _moe(x, comb, exp_w_gate[l].astype(bf), exp_w_up[l].astype(bf), exp_w_down[l].astype(bf),
                 row(ln2_g[l]), row(ln2_b[l]), alpha)
    return x.reshape(B, S, D)
```

```python
import dataclasses
import functools
import math

import jax
import jax.numpy as jnp
from jax import lax
from jax.experimental import pallas as pl
from jax.experimental.pallas import tpu as pltpu
from jax.experimental.pallas import tpu_sc as plsc

D_MODEL = 1024
D_RNN = 1024
RNN_BLOCKS = 16
RNN_BW = D_RNN // RNN_BLOCKS
CONV_W = 4
LRU_C = 8.0
N_HEADS = 8
HEAD_DIM = 128
D_ATT = N_HEADS * HEAD_DIM
KV_RANK = 256
IDX_HEADS = 8
IDX_DIM = 64
TOPK_MAX = 256
N_BUCKETS = 32
MAX_DIST = 128
N_EXPERTS = 16
N_GROUPS = 4
EXPERTS_PER_GROUP = N_EXPERTS // N_GROUPS
D_EXPERT = 512
LN_EPS = 1e-5
RMS_EPS = 1e-6

COL_XRNN = 0
COL_GRNN = 1024
COL_Q = 2048
COL_GATE_RNN = 3072
COL_GATE_ATT = 4096
COL_QIDX = 5120
COL_CKV = 5632
COL_KIDX = 5888
D_IN_PAD = 6144

SC_CORES = 2
SC_SUBCORES = 16
SC_LANES = 16
SC_WORKERS = SC_CORES * SC_SUBCORES
SC_MAX_INDEX_ROW = 128

INT_MIN = -2 ** 31
VMEM_LIMIT = 56 * 1024 * 1024


def _cparams(*sem):
    return pltpu.CompilerParams(dimension_semantics=sem, vmem_limit_bytes=VMEM_LIMIT)


def _layer_norm(v, g, b):
    mu = jnp.mean(v, axis=-1, keepdims=True)
    var = jnp.mean(jnp.square(v - mu), axis=-1, keepdims=True)
    return (v - mu) * lax.rsqrt(var + LN_EPS) * g + b


def _inproj_kernel(x_ref, w_ref, o_ref, xb_ref):
    @pl.when(pl.program_id(1) == 0)
    def _():
        xb_ref[...] = x_ref[...].astype(jnp.bfloat16)

    o_ref[...] = jnp.dot(xb_ref[...], w_ref[...], preferred_element_type=jnp.float32)


def _inproj(x, w, tm=1024, tn=512):
    T, K = x.shape
    N = w.shape[1]
    return pl.pallas_call(
        _inproj_kernel,
        grid=(T // tm, N // tn),
        in_specs=[pl.BlockSpec((tm, K), lambda i, j: (i, 0)),
                  pl.BlockSpec((K, tn), lambda i, j: (0, j))],
        out_specs=pl.BlockSpec((tm, tn), lambda i, j: (i, j)),
        out_shape=jax.ShapeDtypeStruct((T, N), jnp.float32),
        scratch_shapes=[pltpu.VMEM((tm, K), jnp.bfloat16)],
        compiler_params=_cparams("parallel", "arbitrary"),
        name="inproj",
    )(x, w)


def _rglru_kernel(x_ref, g_ref, cw_ref, cb_ref, wa_ref, ba_ref, wx_ref, bx_ref, lam_ref,
                  o_ref, prev_ref, carry_ref, a_ref, u_ref):
    ts = x_ref.shape[0]

    @pl.when(pl.program_id(1) == 0)
    def _():
        prev_ref[...] = jnp.zeros_like(prev_ref)
        carry_ref[...] = jnp.zeros_like(carry_ref)

    x = x_ref[...]
    xe = jnp.concatenate([prev_ref[...], x], axis=0)
    xr = cb_ref[...] + sum(cw_ref[k:k + 1, :] * xe[5 + k:5 + k + ts, :] for k in range(CONV_W))
    prev_ref[...] = x[ts - 8:, :]

    xb = xr.astype(jnp.bfloat16)
    nt = D_RNN // 256
    ra = jnp.concatenate([jnp.dot(xb[:, c * 256:(c + 1) * 256], wa_ref[c],
                                  preferred_element_type=jnp.float32) for c in range(nt)], axis=1)
    rx = jnp.concatenate([jnp.dot(xb[:, c * 256:(c + 1) * 256], wx_ref[c],
                                  preferred_element_type=jnp.float32) for c in range(nt)], axis=1)
    r = jax.nn.sigmoid(ra + ba_ref[...])
    gi = jax.nn.sigmoid(rx + bx_ref[...])
    z = -lam_ref[...]
    softplus = jnp.maximum(z, 0.0) + jnp.log(1.0 + jnp.exp(-jnp.abs(z)))
    log_a = (-LRU_C * r) * softplus
    a_ref[...] = jnp.exp(log_a)
    u_ref[...] = jnp.sqrt(1.0 - jnp.exp(2.0 * log_a)) * (gi * xr)

    row = lax.broadcasted_iota(jnp.int32, (8, D_RNN), 0)

    def group(gidx, carry):
        r0 = pl.multiple_of(gidx * 8, 8)
        a8 = a_ref[pl.ds(r0, 8), :]
        u8 = u_ref[pl.ds(r0, 8), :]
        for d in (1, 2, 4):
            keep = row >= d
            a_sh = pltpu.roll(a8, d, 0)
            u_sh = pltpu.roll(u8, d, 0)
            u8 = jnp.where(keep, a8 * u_sh + u8, u8)
            a8 = jnp.where(keep, a8 * a_sh, a8)
        h8 = a8 * carry + u8
        u_ref[pl.ds(r0, 8), :] = h8
        return h8[7:8, :]

    carry_ref[...] = lax.fori_loop(0, ts // 8, group, carry_ref[...], unroll=4)
    o_ref[...] = (u_ref[...] * jax.nn.gelu(g_ref[...])).astype(o_ref.dtype)


def _rglru(h_all, B, S, cw, cb, wa, ba, wx, bx, lam, ts=256):
    nblk = S // ts
    row = lambda b, i: b * nblk + i
    vec = pl.BlockSpec((1, D_RNN), lambda b, i: (0, 0))
    tile = pl.BlockSpec((D_RNN // 256, 256, 256), lambda b, i: (0, 0, 0))
    return pl.pallas_call(
        _rglru_kernel,
        grid=(B, nblk),
        in_specs=[pl.BlockSpec((ts, D_RNN), lambda b, i: (row(b, i), COL_XRNN // D_RNN)),
                  pl.BlockSpec((ts, D_RNN), lambda b, i: (row(b, i), COL_GRNN // D_RNN)),
                  pl.BlockSpec((CONV_W, D_RNN), lambda b, i: (0, 0)),
                  vec, tile, vec, tile, vec, vec],
        out_specs=pl.BlockSpec((ts, D_RNN), lambda b, i: (row(b, i), 0)),
        out_shape=jax.ShapeDtypeStruct((B * S, D_RNN), jnp.bfloat16),
        scratch_shapes=[pltpu.VMEM((8, D_RNN), jnp.float32),
                        pltpu.VMEM((1, D_RNN), jnp.float32),
                        pltpu.VMEM((ts, D_RNN), jnp.float32),
                        pltpu.VMEM((ts, D_RNN), jnp.float32)],
        compiler_params=_cparams("arbitrary", "arbitrary"),
        name="rglru",
    )(h_all, h_all, cw, cb, wa, ba, wx, bx, lam)


def _kvpack_kernel(c_ref, g_ref, o_ref):
    c = c_ref[...]
    cn = c * lax.rsqrt(jnp.mean(jnp.square(c), axis=-1, keepdims=True) + RMS_EPS) * g_ref[...]
    cb = cn.astype(jnp.bfloat16).astype(jnp.float32)
    half = KV_RANK // 2
    lo = lax.bitcast_convert_type(cb[:, :half], jnp.int32)
    hi = lax.bitcast_convert_type(cb[:, half:], jnp.int32)
    o_ref[...] = (hi & jnp.int32(-65536)) | lax.shift_right_logical(lo, 16)


def _kvpack(h_all, kv_norm, ts=1024):
    T = h_all.shape[0]
    return pl.pallas_call(
        _kvpack_kernel,
        grid=(T // ts,),
        in_specs=[pl.BlockSpec((ts, KV_RANK), lambda i: (i, COL_CKV // KV_RANK)),
                  pl.BlockSpec((1, KV_RANK), lambda i: (0, 0))],
        out_specs=pl.BlockSpec((ts, KV_RANK // 2), lambda i: (i, 0)),
        out_shape=jax.ShapeDtypeStruct((T, KV_RANK // 2), jnp.int32),
        compiler_params=_cparams("parallel"),
        name="kvpack",
    )(h_all, kv_norm)


def _unpack_kv(w):
    lo = lax.bitcast_convert_type(lax.shift_left(w, 16), jnp.float32)
    hi = lax.bitcast_convert_type(w & jnp.int32(-65536), jnp.float32)
    return jnp.concatenate([lo, hi], axis=-1).astype(jnp.bfloat16)


def _qlat_kernel(q_ref, w_ref, o_ref):
    q = q_ref[...].astype(jnp.bfloat16)
    for h in range(N_HEADS):
        o_ref[:, h, :] = jnp.dot(q[:, h * HEAD_DIM:(h + 1) * HEAD_DIM], w_ref[h],
                                 preferred_element_type=jnp.float32) * (HEAD_DIM ** -0.5)


def _qlat(h_all, w_ukT, tm=256):
    T = h_all.shape[0]
    return pl.pallas_call(
        _qlat_kernel,
        grid=(T // tm,),
        in_specs=[pl.BlockSpec((tm, D_ATT), lambda i: (i, COL_Q // D_ATT)),
                  pl.BlockSpec((N_HEADS, HEAD_DIM, KV_RANK), lambda i: (0, 0, 0))],
        out_specs=pl.BlockSpec((tm, N_HEADS, KV_RANK), lambda i: (i, 0, 0)),
        out_shape=jax.ShapeDtypeStruct((T, N_HEADS, KV_RANK), jnp.float32),
        compiler_params=_cparams("parallel"),
        name="qlat",
    )(h_all, w_ukT)


IDX_QB = 128
IDX_KC = 512


def _indexer_kernel(q_ref, kw_ref, k_ref, keys_ref, lo_ref):
    S = keys_ref.shape[2]
    i = pl.program_id(0)
    n_causal = (i * IDX_QB + IDX_QB - 1) // IDX_KC + 1
    q_all = jnp.concatenate(
        [(q_ref[:, h * IDX_DIM:(h + 1) * IDX_DIM] * (IDX_DIM ** -0.5)).astype(jnp.bfloat16)
         for h in range(IDX_HEADS)], axis=0)
    ws = [kw_ref[:, IDX_DIM + h:IDX_DIM + h + 1] * (IDX_HEADS ** -0.5) for h in range(IDX_HEADS)]
    t = i * IDX_QB + lax.broadcasted_iota(jnp.int32, (IDX_QB, IDX_KC), 0)
    lane = lax.broadcasted_iota(jnp.int32, (IDX_QB, IDX_KC), 1)
    n_slab = IDX_KC // 128

    def score_chunk(c, carry):
        top1, top2 = carry
        c0 = pl.multiple_of(c * IDX_KC, IDX_KC)
        kc = k_ref[0, pl.ds(c0, IDX_KC), :]
        d = lax.dot_general(q_all, kc, (((1,), (1,)), ((), ())), preferred_element_type=jnp.float32)
        sc = jnp.zeros((IDX_QB, IDX_KC), jnp.float32)
        for h in range(IDX_HEADS):
            sc = sc + jnp.maximum(d[h * IDX_QB:(h + 1) * IDX_QB], 0.0) * ws[h]
        bits = lax.bitcast_convert_type(sc, jnp.int32)
        key = bits ^ (lax.shift_right_arithmetic(bits, 31) & jnp.int32(0x7FFFFFFF))
        key = jnp.where(c0 + lane <= t, key, jnp.int32(INT_MIN))
        keys_ref[0, :, pl.ds(c0, IDX_KC)] = key
        for j in range(n_slab):
            x = key[:, j * 128:(j + 1) * 128]
            top2 = jnp.maximum(top2, jnp.minimum(top1, x))
            top1 = jnp.maximum(top1, x)
        return top1, top2

    floor = jnp.full((IDX_QB, 128), INT_MIN, jnp.int32)
    top1, top2 = lax.fori_loop(0, n_causal, score_chunk, (floor, floor))

    def blank_chunk(c, _):
        c0 = pl.multiple_of(c * IDX_KC, IDX_KC)
        keys_ref[0, :, pl.ds(c0, IDX_KC)] = jnp.full((IDX_QB, IDX_KC), INT_MIN, jnp.int32)
        return 0

    lax.fori_loop(n_causal, S // IDX_KC, blank_chunk, 0)

    lo = jnp.maximum(jnp.min(top2, axis=1, keepdims=True), jnp.int32(INT_MIN + 1))
    lo_ref[0] = jnp.broadcast_to(lo, (IDX_QB, 128))


def _indexer(h_all, kidx, b, S):
    nblk = S // IDX_QB
    return pl.pallas_call(
        _indexer_kernel,
        grid=(nblk,),
        in_specs=[pl.BlockSpec((IDX_QB, IDX_HEADS * IDX_DIM), lambda i: (b * nblk + i, COL_QIDX // 512)),
                  pl.BlockSpec((IDX_QB, 128), lambda i: (b * nblk + i, COL_KIDX // 128)),
                  pl.BlockSpec((1, S, IDX_DIM), lambda i: (b, 0, 0))],
        out_specs=[pl.BlockSpec((1, IDX_QB, S), lambda i: (0, i, 0)),
                   pl.BlockSpec((1, IDX_QB, 128), lambda i: (0, i, 0))],
        out_shape=[jax.ShapeDtypeStruct((1, S, S), jnp.int32),
                   jax.ShapeDtypeStruct((1, S, 128), jnp.int32)],
        compiler_params=_cparams("parallel"),
        name="indexer",
    )(h_all, h_all, kidx)


SC_SEG = 4096
SC_VALUE_STEPS = 40


def _select_gather(keys, lo_w, table, base, n_keep):
    T, S = keys.shape
    W = table.shape[1]
    L = SC_LANES
    per_worker = T // SC_WORKERS
    n_idx_rows = n_keep // SC_MAX_INDEX_ROW
    mesh = plsc.VectorSubcoreMesh(core_axis_name="c", subcore_axis_name="s")
    cp = pltpu.CompilerParams()
    if "needs_layout_passes" in pltpu.CompilerParams.__dataclass_fields__:
        cp = dataclasses.replace(cp, needs_layout_passes=False)

    seg = min(SC_SEG, S)
    n_seg = S // seg
    unroll = 8
    per_row = SC_MAX_INDEX_ROW // L

    @functools.partial(
        pl.kernel, mesh=mesh, compiler_params=cp,
        out_type=(jax.ShapeDtypeStruct((T, n_idx_rows, SC_MAX_INDEX_ROW), jnp.int32),
                  jax.ShapeDtypeStruct((T, n_keep, W), jnp.int32)),
        scratch_types=[pltpu.VMEM((2, S), jnp.int32),
                       pltpu.VMEM((per_worker,), jnp.int32),
                       pltpu.VMEM((S,), jnp.int32),
                       pltpu.VMEM((n_idx_rows, SC_MAX_INDEX_ROW), jnp.int32),
                       pltpu.VMEM((2, n_idx_rows, SC_MAX_INDEX_ROW), jnp.int32),
                       pltpu.VMEM((2, n_keep, W), jnp.int32),
                       pltpu.SemaphoreType.DMA((2,)),
                       pltpu.SemaphoreType.DMA((2,)),
                       pltpu.SemaphoreType.DMA((2,))],
        name="select_gather",
    )
    def body(keys_hbm, lo_hbm, table_hbm, idx_hbm, rows_hbm,
             krow, lo_v, cand, idx_s, idx_g, rows_v, key_sem, gat_sem, out_sem):
        wid = lax.axis_index("s") * SC_CORES + lax.axis_index("c")
        lane = lax.iota(jnp.int32, L)
        zero = jnp.zeros((L,), jnp.int32)

        def key_copy(r, k, buf):
            return pltpu.make_async_copy(keys_hbm.at[r, pl.ds(k * seg, seg)],
                                         krow.at[buf, pl.ds(k * seg, seg)], key_sem.at[buf])

        def gather_copy(h, buf):
            return pltpu.make_async_copy(
                table_hbm.at[idx_g.at[buf, h]],
                rows_v.at[buf, pl.ds(h * SC_MAX_INDEX_ROW, SC_MAX_INDEX_ROW)], gat_sem.at[buf])

        def out_copies(r, buf):
            return (pltpu.make_async_copy(rows_v.at[buf], rows_hbm.at[r], out_sem.at[buf]),
                    pltpu.make_async_copy(idx_g.at[buf], idx_hbm.at[r], out_sem.at[buf]))

        def row_of(q):
            r = wid + SC_WORKERS * q
            return r, r, base

        def fetch_keys(q, buf):
            r, t, _ = row_of(q)
            for k in range(n_seg):
                @pl.when((q < per_worker) & (t >= k * seg))
                def _():
                    key_copy(r, k, buf).start()

        def step(q, buf):
            r, t, row0 = row_of(q)
            fetch_keys(q + 1, 1 - buf)
            for k in range(n_seg):
                @pl.when(t >= k * seg)
                def _():
                    key_copy(r, k, buf).wait()
            lo0 = plsc.load_gather(lo_v, [zero + q])
            fill = zero + row0
            for j in range(n_keep // L):
                idx_s[j // per_row, pl.ds((j % per_row) * L, L)] = fill

            def ones(m):
                return jnp.where(m, 1, 0).astype(jnp.int32)

            def pick(g, carry):
                cnt, top = carry
                vs = [krow[buf, pl.ds((g * unroll + u) * L, L)] for u in range(unroll)]
                for u, v in enumerate(vs):
                    m = v >= lo0
                    pos = cnt + plsc.cumsum(ones(m)) - 1
                    plsc.store_scatter(cand, [pos], (g * unroll + u) * L + lane, mask=m)
                    cnt = cnt + plsc.all_reduce_population_count(m)
                    top = jnp.maximum(top, v)
                return cnt, top

            n_cand, top = plsc.parallel_loop(0, t // (unroll * L) + 1,
                                             carry=(zero, zero + INT_MIN))(pick)
            n_vec = (jnp.max(n_cand) + L - 1) // L

            def cand_keys(j):
                ok = j * L + lane < n_cand
                pos = jnp.where(ok, cand[pl.ds(j * L, L)], 0)
                return plsc.load_gather(krow, [zero + buf, pos]), pos, ok

            def count(pred):
                def one(j, acc):
                    kv, _, ok = cand_keys(j)
                    return acc + plsc.all_reduce_population_count(pred(kv) & ok)
                return lax.fori_loop(0, n_vec, one, zero)

            search = n_cand > n_keep

            def active(lo, hi, cnt):
                return search & (cnt != n_keep) & (lo + 1 < hi)

            def flip(v):
                return v ^ (lax.shift_right_arithmetic(v, 31) & jnp.int32(0x7FFFFFFF))

            def probe(lo, hi, step):
                mid = (lo & hi) + lax.shift_right_arithmetic(lo ^ hi, 1)
                mean = (0.5 * lax.bitcast_convert_type(flip(lo), jnp.float32)
                        + 0.5 * lax.bitcast_convert_type(flip(hi), jnp.float32))
                mid_v = flip(lax.bitcast_convert_type(mean, jnp.int32))
                width = hi - lo
                far = (width < 0) | (width > (1 << 24))
                mid = jnp.where(far & (step < SC_VALUE_STEPS) & (mid_v > lo) & (mid_v < hi), mid_v, mid)
                mid = jnp.where((lo < 0) & (hi > 0), 0, mid)
                return jnp.where((lo == 0) & (hi > 1), 1, mid)

            def search_body(carry):
                lo, hi, cnt, step, _ = carry
                mid = probe(lo, hi, step)
                n = count(lambda kv: kv >= mid)
                up = n >= n_keep
                lo, hi, cnt = jnp.where(up, mid, lo), jnp.where(up, hi, mid), jnp.where(up, n, cnt)
                return lo, hi, cnt, step + 1, jnp.max(ones(active(lo, hi, cnt)))

            hi0 = zero + jnp.max(top) + 1
            lo, hi, cnt, _, _ = lax.while_loop(
                lambda carry: carry[4] > 0, search_body,
                (lo0, hi0, n_cand, jnp.int32(0), jnp.max(ones(active(lo0, hi0, n_cand)))))

            tie = search & (cnt != n_keep)
            thr = jnp.where(tie, lo, lo - 1)
            n_eq = jnp.where(tie, n_keep - count(lambda kv: kv > lo), 0)

            def emit(j, carry):
                cnt, seen = carry
                kv, pos, ok = cand_keys(j)
                eq = (kv == thr) & ok
                rank = seen + plsc.cumsum(ones(eq))
                m = ok & ((kv > thr) | (eq & (rank <= n_eq)))
                slot = cnt + plsc.cumsum(ones(m)) - 1
                fits = m & (slot < n_keep)
                slot = jnp.minimum(slot, n_keep - 1)
                plsc.store_scatter(idx_s, [slot >> 7, slot & 127], pos + row0, mask=fits)
                return (cnt + plsc.all_reduce_population_count(m),
                        seen + plsc.all_reduce_population_count(eq))

            lax.fori_loop(0, n_vec, emit, (zero, zero))

            @pl.when(q >= 2)
            def _():
                for c in out_copies(r, buf):
                    c.wait()

            for j in range(n_keep // L):
                sl = (j // per_row, pl.ds((j % per_row) * L, L))
                idx_g[(buf,) + sl] = idx_s[sl]

            @pl.when(q >= 1)
            def _():
                for h in range(n_idx_rows):
                    gather_copy(h, 1 - buf).wait()
                for c in out_copies(r - SC_WORKERS, 1 - buf):
                    c.start()

            for h in range(n_idx_rows):
                gather_copy(h, buf).start()

        pltpu.sync_copy(lo_hbm.at[wid], lo_v)
        fetch_keys(jnp.int32(0), 0)

        def pair(p, _):
            step(2 * p, 0)
            step(2 * p + 1, 1)
            return 0

        lax.fori_loop(0, per_worker // 2, pair, 0)

        r_last = row_of(per_worker - 1)[0]
        for h in range(n_idx_rows):
            gather_copy(h, 1).wait()
        for c in out_copies(r_last, 1):
            c.start()
        for c in out_copies(r_last, 0) + out_copies(r_last, 1):
            c.wait()

    return body(keys, lo_w, table)


def _attn_kernel(ql_ref, kv_ref, idx_ref, tbl_ref, o_ref, kvb_ref, lg_ref, bk_ref, *, base):
    tq = ql_ref.shape[0]
    n_keep = kv_ref.shape[1]
    t0 = pl.program_id(0) * tq
    row0 = base
    max_exact = N_BUCKETS // 2

    t_row = t0 + lax.broadcasted_iota(jnp.int32, (tq, n_keep), 0)
    d = jnp.maximum(t_row - (idx_ref[...] - row0), 0)
    large = max_exact + (jnp.log(jnp.maximum(d, 1).astype(jnp.float32) / max_exact)
                         / math.log(MAX_DIST / max_exact) * (N_BUCKETS - max_exact)).astype(jnp.int32)
    bk_ref[...] = jnp.where(d < max_exact, d, jnp.minimum(large, N_BUCKETS - 1))
    bucket_id = lax.broadcasted_iota(jnp.int32, (N_BUCKETS, n_keep), 0)

    def logits_of(q, _):
        kv = _unpack_kv(kv_ref[q])
        kvb_ref[q] = kv
        onehot = jnp.where(bk_ref[pl.ds(q, 1), :] == bucket_id, 1.0, 0.0).astype(jnp.bfloat16)
        bias2 = jnp.dot(tbl_ref[...], onehot, preferred_element_type=jnp.float32)
        ql = ql_ref[q].astype(jnp.bfloat16)
        lg_ref[q] = (lax.dot_general(ql, kv, (((1,), (1,)), ((), ())), preferred_element_type=jnp.float32)
                     + bias2[:N_HEADS] + bias2[N_HEADS:])
        return 0

    lax.fori_loop(0, tq, logits_of, 0, unroll=4)

    shape = (tq, N_HEADS, n_keep)
    valid = lax.broadcasted_iota(jnp.int32, shape, 2) <= t0 + lax.broadcasted_iota(jnp.int32, shape, 0)
    logits = jnp.where(valid, lg_ref[...], -1e30)
    e = jnp.exp(logits - jnp.max(logits, axis=-1, keepdims=True))
    lg_ref[...] = e / jnp.sum(e, axis=-1, keepdims=True)

    def values_of(q, _):
        o_ref[q] = jnp.dot(lg_ref[q].astype(jnp.bfloat16), kvb_ref[q], preferred_element_type=jnp.float32)
        return 0

    lax.fori_loop(0, tq, values_of, 0, unroll=4)


def _attn(qlat, rows, idx, bias_tab, b, tq=32):
    T, n_keep, W = rows.shape
    nblk = T // tq
    return pl.pallas_call(
        functools.partial(_attn_kernel, base=b * T),
        grid=(nblk,),
        in_specs=[pl.BlockSpec((tq, N_HEADS, KV_RANK), lambda i: (b * nblk + i, 0, 0)),
                  pl.BlockSpec((tq, n_keep, W), lambda i: (i, 0, 0)),
                  pl.BlockSpec((tq, n_keep), lambda i: (i, 0)),
                  pl.BlockSpec((2 * N_HEADS, N_BUCKETS), lambda i: (0, 0))],
        out_specs=pl.BlockSpec((tq, N_HEADS, KV_RANK), lambda i: (i, 0, 0)),
        out_shape=jax.ShapeDtypeStruct((T, N_HEADS, KV_RANK), jnp.float32),
        scratch_shapes=[pltpu.VMEM((tq, n_keep, KV_RANK), jnp.bfloat16),
                        pltpu.VMEM((tq, N_HEADS, n_keep), jnp.float32),
                        pltpu.VMEM((tq, n_keep), jnp.int32)],
        compiler_params=_cparams("parallel"),
        name="attn",
    )(qlat, rows, idx, bias_tab)


def _oproj_kernel(o_ref, w_ref, y_ref):
    for h in range(N_HEADS):
        y_ref[:, h * HEAD_DIM:(h + 1) * HEAD_DIM] = jnp.dot(
            o_ref[:, h, :].astype(jnp.bfloat16), w_ref[h],
            preferred_element_type=jnp.float32).astype(y_ref.dtype)


def _oproj(o_lat, w_uv, tm=256):
    T = o_lat.shape[0]
    return pl.pallas_call(
        _oproj_kernel,
        grid=(T // tm,),
        in_specs=[pl.BlockSpec((tm, N_HEADS, KV_RANK), lambda i: (i, 0, 0)),
                  pl.BlockSpec((N_HEADS, KV_RANK, HEAD_DIM), lambda i: (0, 0, 0))],
        out_specs=pl.BlockSpec((tm, D_ATT), lambda i: (i, 0)),
        out_shape=jax.ShapeDtypeStruct((T, D_ATT), jnp.bfloat16),
        compiler_params=_cparams("parallel"),
        name="oproj",
    )(o_lat, w_uv)


def _merge_kernel(x_ref, yr_ref, ya_ref, gr_ref, ga_ref, pr_ref, pa_ref, wo_ref, g_ref, b_ref,
                  o_ref, *, alpha):
    a = jnp.dot(yr_ref[...], pr_ref[...], preferred_element_type=jnp.float32)
    c = jnp.dot(ya_ref[...], pa_ref[...], preferred_element_type=jnp.float32)
    merged = jax.nn.sigmoid(gr_ref[...]) * a + jax.nn.sigmoid(ga_ref[...]) * c
    mix = jnp.dot(merged.astype(jnp.bfloat16), wo_ref[...], preferred_element_type=jnp.float32)
    o_ref[...] = _layer_norm(alpha * x_ref[...] + mix, g_ref[...], b_ref[...])


def _merge(x, y_rnn, y_att, h_all, proj_rnn, proj_att, w_out, g, b, alpha, tm=512):
    T = x.shape[0]
    tok = lambda c: pl.BlockSpec((tm, D_MODEL), lambda i: (i, c))
    wgt = pl.BlockSpec((D_MODEL, D_MODEL), lambda i: (0, 0))
    vec = pl.BlockSpec((1, D_MODEL), lambda i: (0, 0))
    return pl.pallas_call(
        functools.partial(_merge_kernel, alpha=alpha),
        grid=(T // tm,),
        in_specs=[tok(0), tok(0), tok(0), tok(COL_GATE_RNN // D_MODEL), tok(COL_GATE_ATT // D_MODEL),
                  wgt, wgt, wgt, vec, vec],
        out_specs=tok(0),
        out_shape=jax.ShapeDtypeStruct((T, D_MODEL), jnp.float32),
        compiler_params=_cparams("parallel"),
        name="merge",
    )(x, y_rnn, y_att, h_all, h_all, proj_rnn, proj_att, w_out, g, b)


def _router_kernel(x_ref, w_ref, b_ref, o_ref):
    logits = lax.dot_general(w_ref[...], x_ref[...], (((1,), (1,)), ((), ())),
                             precision=lax.Precision.HIGHEST,
                             preferred_element_type=jnp.float32)
    aff = [jax.nn.sigmoid(logits[e:e + 1, :]) for e in range(N_EXPERTS)]
    sel = [aff[e] + b_ref[e:e + 1, :] for e in range(N_EXPERTS)]
    P = EXPERTS_PER_GROUP
    gscore = []
    for g in range(N_GROUPS):
        v = sel[g * P:(g + 1) * P]
        best = None
        for a in range(P):
            for c in range(a + 1, P):
                s2 = v[a] + v[c]
                best = s2 if best is None else jnp.maximum(best, s2)
        gscore.append(best)
    gbest = jnp.zeros_like(gscore[0], dtype=jnp.int32)
    gmax = gscore[0]
    for g in range(1, N_GROUPS):
        better = gscore[g] > gmax
        gbest = jnp.where(better, g, gbest)
        gmax = jnp.where(better, gscore[g], gmax)

    def pick(vals):
        out = vals[0]
        for g in range(1, N_GROUPS):
            out = jnp.where(gbest == g, vals[g], out)
        return out

    sv = [pick([sel[g * P + j] for g in range(N_GROUPS)]) for j in range(P)]
    av = [pick([aff[g * P + j] for g in range(N_GROUPS)]) for j in range(P)]

    def first_max(vals, skip=None):
        idx = None
        best = None
        for j in range(P):
            v = vals[j] if skip is None else jnp.where(skip == j, -jnp.inf, vals[j])
            if best is None:
                idx, best = jnp.zeros_like(gbest), v
            else:
                better = v > best
                idx = jnp.where(better, j, idx)
                best = jnp.where(better, v, best)
        return idx

    j1 = first_max(sv)
    j2 = first_max(sv, skip=j1)
    g1 = av[0]
    g2 = av[0]
    for j in range(1, P):
        g1 = jnp.where(j1 == j, av[j], g1)
        g2 = jnp.where(j2 == j, av[j], g2)
    den = g1 + g2
    rows = []
    for e in range(N_EXPERTS):
        g, j = divmod(e, P)
        w = jnp.where(j1 == j, g1 / den, jnp.where(j2 == j, g2 / den, 0.0))
        rows.append(jnp.where(gbest == g, w, 0.0))
    o_ref[...] = jnp.concatenate(rows, axis=0)


def _router(x, w_routerT, bias, tm=512):
    T = x.shape[0]
    return pl.pallas_call(
        _router_kernel,
        grid=(T // tm,),
        in_specs=[pl.BlockSpec((tm, D_MODEL), lambda i: (i, 0)),
                  pl.BlockSpec((N_EXPERTS, D_MODEL), lambda i: (0, 0)),
                  pl.BlockSpec((N_EXPERTS, 1), lambda i: (0, 0))],
        out_specs=pl.BlockSpec((N_EXPERTS, tm), lambda i: (0, i)),
        out_shape=jax.ShapeDtypeStruct((N_EXPERTS, T), jnp.float32),
        compiler_params=_cparams("parallel"),
        name="router",
    )(x, w_routerT, bias)


def _moe_kernel(x_ref, c_ref, wg_ref, wu_ref, wd_ref, g_ref, b_ref, o_ref, xb_ref, acc_ref, *, alpha):
    e = pl.program_id(1)

    @pl.when(e == 0)
    def _():
        xb_ref[...] = x_ref[...].astype(jnp.bfloat16)
        acc_ref[...] = jnp.zeros_like(acc_ref)

    xb = xb_ref[...]
    gate = jnp.dot(xb, wg_ref[0], preferred_element_type=jnp.float32)
    up = jnp.dot(xb, wu_ref[0], preferred_element_type=jnp.float32)
    he = (jax.nn.silu(gate) * up).astype(jnp.bfloat16)
    dn = jnp.dot(he, wd_ref[0], preferred_element_type=jnp.float32)
    lane = lax.broadcasted_iota(jnp.int32, c_ref.shape, 1)
    ce = jnp.sum(jnp.where(lane == e, c_ref[...], 0.0), axis=1, keepdims=True)
    acc_ref[...] += ce * dn

    @pl.when(e == N_EXPERTS - 1)
    def _():
        o_ref[...] = _layer_norm(alpha * x_ref[...] + acc_ref[...], g_ref[...], b_ref[...])


def _moe(x, comb, w_gate, w_up, w_down, g, b, alpha, tm=512):
    T = x.shape[0]
    vec = pl.BlockSpec((1, D_MODEL), lambda i, e: (0, 0))
    return pl.pallas_call(
        functools.partial(_moe_kernel, alpha=alpha),
        grid=(T // tm, N_EXPERTS),
        in_specs=[pl.BlockSpec((tm, D_MODEL), lambda i, e: (i, 0)),
                  pl.BlockSpec((tm, N_EXPERTS), lambda i, e: (i, 0)),
                  pl.BlockSpec((1, D_MODEL, D_EXPERT), lambda i, e: (e, 0, 0)),
                  pl.BlockSpec((1, D_MODEL, D_EXPERT), lambda i, e: (e, 0, 0)),
                  pl.BlockSpec((1, D_EXPERT, D_MODEL), lambda i, e: (e, 0, 0)),
                  vec, vec],
        out_specs=pl.BlockSpec((tm, D_MODEL), lambda i, e: (i, 0)),
        out_shape=jax.ShapeDtypeStruct((T, D_MODEL), jnp.float32),
        scratch_shapes=[pltpu.VMEM((tm, D_MODEL), jnp.bfloat16),
                        pltpu.VMEM((tm, D_MODEL), jnp.float32)],
        compiler_params=_cparams("parallel", "arbitrary"),
        name="moe",
    )(x, comb, w_gate, w_up, w_down, g, b)


def _pack_w_in(w):
    xr, gr, q, ckv, qi, ki, wi, gate_r, gate_a = jnp.split(
        w, [1024, 2048, 3072, 3328, 3840, 3904, 3912, 4936], axis=1)
    pad = jnp.zeros((w.shape[0], D_IN_PAD - w.shape[1]), w.dtype)
    return jnp.concatenate([xr, gr, q, gate_r, gate_a, qi, ckv, ki, wi, pad], axis=1).astype(jnp.bfloat16)


def _block_diag_tiles(w):
    per = 256 // RNN_BW
    w = w.reshape(D_RNN // 256, per, RNN_BW, RNN_BW)
    eye = jnp.eye(per, dtype=w.dtype)
    return jnp.einsum('cgij,gh->cgihj', w, eye).reshape(D_RNN // 256, 256, 256).astype(jnp.bfloat16)


def kernel(x, w_in, conv_w, conv_b, lru_wa, lru_ba, lru_wx, lru_bx, lru_lambda, kv_norm, w_uk, w_uv, proj_rnn, proj_att, w_out, ln1_g, ln1_b, w_router, router_bias, exp_w_gate, exp_w_up, exp_w_down, ln2_g, ln2_b, rel_bias):
    B, S, D = x.shape
    T = B * S
    depth = w_in.shape[0]
    alpha = (2 * depth) ** 0.25
    n_keep = min(TOPK_MAX, S // 4)
    bf = jnp.bfloat16
    row = lambda v: v.reshape(1, -1)

    bias_hi = rel_bias.T.astype(bf)
    bias_lo = (rel_bias.T - bias_hi.astype(jnp.float32)).astype(bf)
    bias_tab = jnp.concatenate([bias_hi, bias_lo], axis=0)
    w_routerT = w_router.T
    rbias = router_bias.reshape(N_EXPERTS, 1)

    x = x.reshape(T, D)
    for l in range(depth):
        h_all = _inproj(x, _pack_w_in(w_in[l]))
        y_rnn = _rglru(h_all, B, S, conv_w[l], row(conv_b[l]), _block_diag_tiles(lru_wa[l]), row(lru_ba[l]),
                       _block_diag_tiles(lru_wx[l]), row(lru_bx[l]), row(lru_lambda[l]))
        table = _kvpack(h_all, row(kv_norm[l]))
        qlat = _qlat(h_all, jnp.swapaxes(w_uk[l], 1, 2).astype(bf))
        kidx = h_all[:, COL_KIDX:COL_KIDX + IDX_DIM].astype(bf).reshape(B, S, IDX_DIM)
        gathered = []
        for b in range(B):
            keys, lo = _indexer(h_all, kidx, b, S)
            lo_w = lo[0, :, 0].reshape(S // SC_WORKERS, SC_WORKERS).T
            gathered.append(_select_gather(keys[0], lo_w, table, b * S, n_keep))
        w_uv_l = w_uv[l].astype(bf)
        y_att = jnp.concatenate(
            [_oproj(_attn(qlat, rows, idx.reshape(S, n_keep), bias_tab, b), w_uv_l)
             for b, (idx, rows) in enumerate(gathered)], axis=0)
        x = _merge(x, y_rnn, y_att, h_all, proj_rnn[l].astype(bf), proj_att[l].astype(bf),
                   w_out[l].astype(bf), row(ln1_g[l]), row(ln1_b[l]), alpha)
        comb = _router(x, w_routerT, rbias).T
        x = _moe(x, comb, exp_w_gate[l].astype(bf), exp_w_up[l].astype(bf), exp_w_down[l].astype(bf),
                 row(ln2_g[l]), row(ln2_b[l]), alpha)
    return x.reshape(B, S, D)
```

```python
import dataclasses
import functools
import math

import jax
import jax.numpy as jnp
from jax import lax
from jax.experimental import pallas as pl
from jax.experimental.pallas import tpu as pltpu
from jax.experimental.pallas import tpu_sc as plsc

D_MODEL = 1024
D_RNN = 1024
RNN_BLOCKS = 16
RNN_BW = D_RNN // RNN_BLOCKS
CONV_W = 4
LRU_C = 8.0
N_HEADS = 8
HEAD_DIM = 128
D_ATT = N_HEADS * HEAD_DIM
KV_RANK = 256
IDX_HEADS = 8
IDX_DIM = 64
TOPK_MAX = 256
N_BUCKETS = 32
MAX_DIST = 128
N_EXPERTS = 16
N_GROUPS = 4
EXPERTS_PER_GROUP = N_EXPERTS // N_GROUPS
D_EXPERT = 512
LN_EPS = 1e-5
RMS_EPS = 1e-6

COL_XRNN = 0
COL_GRNN = 1024
COL_Q = 2048
COL_GATE_RNN = 3072
COL_GATE_ATT = 4096
COL_QIDX = 5120
COL_CKV = 5632
COL_KIDX = 5888
D_IN_PAD = 6144

SC_CORES = 2
SC_SUBCORES = 16
SC_LANES = 16
SC_WORKERS = SC_CORES * SC_SUBCORES
SC_MAX_INDEX_ROW = 128

INT_MIN = -2 ** 31
VMEM_LIMIT = 56 * 1024 * 1024


def _cparams(*sem):
    return pltpu.CompilerParams(dimension_semantics=sem, vmem_limit_bytes=VMEM_LIMIT)


def _layer_norm(v, g, b):
    mu = jnp.mean(v, axis=-1, keepdims=True)
    var = jnp.mean(jnp.square(v - mu), axis=-1, keepdims=True)
    return (v - mu) * lax.rsqrt(var + LN_EPS) * g + b


def _inproj_kernel(x_ref, w_ref, o_ref, xb_ref):
    @pl.when(pl.program_id(1) == 0)
    def _():
        xb_ref[...] = x_ref[...].astype(jnp.bfloat16)

    o_ref[...] = jnp.dot(xb_ref[...], w_ref[...], preferred_element_type=jnp.float32)


def _inproj(x, w, tm=1024, tn=512):
    T, K = x.shape
    N = w.shape[1]
    return pl.pallas_call(
        _inproj_kernel,
        grid=(T // tm, N // tn),
        in_specs=[pl.BlockSpec((tm, K), lambda i, j: (i, 0)),
                  pl.BlockSpec((K, tn), lambda i, j: (0, j))],
        out_specs=pl.BlockSpec((tm, tn), lambda i, j: (i, j)),
        out_shape=jax.ShapeDtypeStruct((T, N), jnp.float32),
        scratch_shapes=[pltpu.VMEM((tm, K), jnp.bfloat16)],
        compiler_params=_cparams("parallel", "arbitrary"),
        name="inproj",
    )(x, w)


def _rglru_kernel(x_ref, g_ref, cw_ref, cb_ref, wa_ref, ba_ref, wx_ref, bx_ref, lam_ref,
                  o_ref, prev_ref, carry_ref, a_ref, u_ref):
    ts = x_ref.shape[0]

    @pl.when(pl.program_id(1) == 0)
    def _():
        prev_ref[...] = jnp.zeros_like(prev_ref)
        carry_ref[...] = jnp.zeros_like(carry_ref)

    x = x_ref[...]
    xe = jnp.concatenate([prev_ref[...], x], axis=0)
    xr = cb_ref[...] + sum(cw_ref[k:k + 1, :] * xe[5 + k:5 + k + ts, :] for k in range(CONV_W))
    prev_ref[...] = x[ts - 8:, :]

    xb = xr.astype(jnp.bfloat16)
    nt = D_RNN // 256
    ra = jnp.concatenate([jnp.dot(xb[:, c * 256:(c + 1) * 256], wa_ref[c],
                                  preferred_element_type=jnp.float32) for c in range(nt)], axis=1)
    rx = jnp.concatenate([jnp.dot(xb[:, c * 256:(c + 1) * 256], wx_ref[c],
                                  preferred_element_type=jnp.float32) for c in range(nt)], axis=1)
    r = jax.nn.sigmoid(ra + ba_ref[...])
    gi = jax.nn.sigmoid(rx + bx_ref[...])
    z = -lam_ref[...]
    softplus = jnp.maximum(z, 0.0) + jnp.log(1.0 + jnp.exp(-jnp.abs(z)))
    log_a = (-LRU_C * r) * softplus
    a_ref[...] = jnp.exp(log_a)
    u_ref[...] = jnp.sqrt(1.0 - jnp.exp(2.0 * log_a)) * (gi * xr)

    row = lax.broadcasted_iota(jnp.int32, (8, D_RNN), 0)

    def group(gidx, carry):
        r0 = pl.multiple_of(gidx * 8, 8)
        a8 = a_ref[pl.ds(r0, 8), :]
        u8 = u_ref[pl.ds(r0, 8), :]
        for d in (1, 2, 4):
            keep = row >= d
            a_sh = pltpu.roll(a8, d, 0)
            u_sh = pltpu.roll(u8, d, 0)
            u8 = jnp.where(keep, a8 * u_sh + u8, u8)
            a8 = jnp.where(keep, a8 * a_sh, a8)
        h8 = a8 * carry + u8
        u_ref[pl.ds(r0, 8), :] = h8
        return h8[7:8, :]

    carry_ref[...] = lax.fori_loop(0, ts // 8, group, carry_ref[...], unroll=4)
    o_ref[...] = (u_ref[...] * jax.nn.gelu(g_ref[...])).astype(o_ref.dtype)


def _rglru(h_all, B, S, cw, cb, wa, ba, wx, bx, lam, ts=256):
    nblk = S // ts
    row = lambda b, i: b * nblk + i
    vec = pl.BlockSpec((1, D_RNN), lambda b, i: (0, 0))
    tile = pl.BlockSpec((D_RNN // 256, 256, 256), lambda b, i: (0, 0, 0))
    return pl.pallas_call(
        _rglru_kernel,
        grid=(B, nblk),
        in_specs=[pl.BlockSpec((ts, D_RNN), lambda b, i: (row(b, i), COL_XRNN // D_RNN)),
                  pl.BlockSpec((ts, D_RNN), lambda b, i: (row(b, i), COL_GRNN // D_RNN)),
                  pl.BlockSpec((CONV_W, D_RNN), lambda b, i: (0, 0)),
                  vec, tile, vec, tile, vec, vec],
        out_specs=pl.BlockSpec((ts, D_RNN), lambda b, i: (row(b, i), 0)),
        out_shape=jax.ShapeDtypeStruct((B * S, D_RNN), jnp.bfloat16),
        scratch_shapes=[pltpu.VMEM((8, D_RNN), jnp.float32),
                        pltpu.VMEM((1, D_RNN), jnp.float32),
                        pltpu.VMEM((ts, D_RNN), jnp.float32),
                        pltpu.VMEM((ts, D_RNN), jnp.float32)],
        compiler_params=_cparams("arbitrary", "arbitrary"),
        name="rglru",
    )(h_all, h_all, cw, cb, wa, ba, wx, bx, lam)


def _kvpack_kernel(c_ref, g_ref, o_ref):
    c = c_ref[...]
    cn = c * lax.rsqrt(jnp.mean(jnp.square(c), axis=-1, keepdims=True) + RMS_EPS) * g_ref[...]
    cb = cn.astype(jnp.bfloat16).astype(jnp.float32)
    half = KV_RANK // 2
    lo = lax.bitcast_convert_type(cb[:, :half], jnp.int32)
    hi = lax.bitcast_convert_type(cb[:, half:], jnp.int32)
    o_ref[...] = (hi & jnp.int32(-65536)) | lax.shift_right_logical(lo, 16)


def _kvpack(h_all, kv_norm, ts=1024):
    T = h_all.shape[0]
    return pl.pallas_call(
        _kvpack_kernel,
        grid=(T // ts,),
        in_specs=[pl.BlockSpec((ts, KV_RANK), lambda i: (i, COL_CKV // KV_RANK)),
                  pl.BlockSpec((1, KV_RANK), lambda i: (0, 0))],
        out_specs=pl.BlockSpec((ts, KV_RANK // 2), lambda i: (i, 0)),
        out_shape=jax.ShapeDtypeStruct((T, KV_RANK // 2), jnp.int32),
        compiler_params=_cparams("parallel"),
        name="kvpack",
    )(h_all, kv_norm)


def _unpack_kv(w):
    lo = lax.bitcast_convert_type(lax.shift_left(w, 16), jnp.float32)
    hi = lax.bitcast_convert_type(w & jnp.int32(-65536), jnp.float32)
    return jnp.concatenate([lo, hi], axis=-1).astype(jnp.bfloat16)


def _qlat_kernel(q_ref, w_ref, o_ref):
    q = q_ref[...].astype(jnp.bfloat16)
    for h in range(N_HEADS):
        o_ref[:, h, :] = jnp.dot(q[:, h * HEAD_DIM:(h + 1) * HEAD_DIM], w_ref[h],
                                 preferred_element_type=jnp.float32) * (HEAD_DIM ** -0.5)


def _qlat(h_all, w_ukT, tm=256):
    T = h_all.shape[0]
    return pl.pallas_call(
        _qlat_kernel,
        grid=(T // tm,),
        in_specs=[pl.BlockSpec((tm, D_ATT), lambda i: (i, COL_Q // D_ATT)),
                  pl.BlockSpec((N_HEADS, HEAD_DIM, KV_RANK), lambda i: (0, 0, 0))],
        out_specs=pl.BlockSpec((tm, N_HEADS, KV_RANK), lambda i: (i, 0, 0)),
        out_shape=jax.ShapeDtypeStruct((T, N_HEADS, KV_RANK), jnp.float32),
        compiler_params=_cparams("parallel"),
        name="qlat",
    )(h_all, w_ukT)


IDX_QB = 128
IDX_KC = 512


def _indexer_kernel(q_ref, kw_ref, k_ref, keys_ref, lo_ref):
    S = keys_ref.shape[2]
    i = pl.program_id(0)
    n_causal = (i * IDX_QB + IDX_QB - 1) // IDX_KC + 1
    q_all = jnp.concatenate(
        [(q_ref[:, h * IDX_DIM:(h + 1) * IDX_DIM] * (IDX_DIM ** -0.5)).astype(jnp.bfloat16)
         for h in range(IDX_HEADS)], axis=0)
    ws = [kw_ref[:, IDX_DIM + h:IDX_DIM + h + 1] * (IDX_HEADS ** -0.5) for h in range(IDX_HEADS)]
    t = i * IDX_QB + lax.broadcasted_iota(jnp.int32, (IDX_QB, IDX_KC), 0)
    lane = lax.broadcasted_iota(jnp.int32, (IDX_QB, IDX_KC), 1)
    n_slab = IDX_KC // 128

    def score_chunk(c, carry):
        top1, top2 = carry
        c0 = pl.multiple_of(c * IDX_KC, IDX_KC)
        kc = k_ref[0, pl.ds(c0, IDX_KC), :]
        d = lax.dot_general(q_all, kc, (((1,), (1,)), ((), ())), preferred_element_type=jnp.float32)
        sc = jnp.zeros((IDX_QB, IDX_KC), jnp.float32)
        for h in range(IDX_HEADS):
            sc = sc + jnp.maximum(d[h * IDX_QB:(h + 1) * IDX_QB], 0.0) * ws[h]
        bits = lax.bitcast_convert_type(sc, jnp.int32)
        key = bits ^ (lax.shift_right_arithmetic(bits, 31) & jnp.int32(0x7FFFFFFF))
        key = jnp.where(c0 + lane <= t, key, jnp.int32(INT_MIN))
        keys_ref[0, :, pl.ds(c0, IDX_KC)] = key
        for j in range(n_slab):
            x = key[:, j * 128:(j + 1) * 128]
            top2 = jnp.maximum(top2, jnp.minimum(top1, x))
            top1 = jnp.maximum(top1, x)
        return top1, top2

    floor = jnp.full((IDX_QB, 128), INT_MIN, jnp.int32)
    top1, top2 = lax.fori_loop(0, n_causal, score_chunk, (floor, floor))

    def blank_chunk(c, _):
        c0 = pl.multiple_of(c * IDX_KC, IDX_KC)
        keys_ref[0, :, pl.ds(c0, IDX_KC)] = jnp.full((IDX_QB, IDX_KC), INT_MIN, jnp.int32)
        return 0

    lax.fori_loop(n_causal, S // IDX_KC, blank_chunk, 0)

    lo = jnp.maximum(jnp.min(top2, axis=1, keepdims=True), jnp.int32(INT_MIN + 1))
    lo_ref[0] = jnp.broadcast_to(lo, (IDX_QB, 128))


def _indexer(h_all, kidx, b, S):
    nblk = S // IDX_QB
    return pl.pallas_call(
        _indexer_kernel,
        grid=(nblk,),
        in_specs=[pl.BlockSpec((IDX_QB, IDX_HEADS * IDX_DIM), lambda i: (b * nblk + i, COL_QIDX // 512)),
                  pl.BlockSpec((IDX_QB, 128), lambda i: (b * nblk + i, COL_KIDX // 128)),
                  pl.BlockSpec((1, S, IDX_DIM), lambda i: (b, 0, 0))],
        out_specs=[pl.BlockSpec((1, IDX_QB, S), lambda i: (0, i, 0)),
                   pl.BlockSpec((1, IDX_QB, 128), lambda i: (0, i, 0))],
        out_shape=[jax.ShapeDtypeStruct((1, S, S), jnp.int32),
                   jax.ShapeDtypeStruct((1, S, 128), jnp.int32)],
        compiler_params=_cparams("parallel"),
        name="indexer",
    )(h_all, h_all, kidx)


SC_SEG = 4096
SC_VALUE_STEPS = 40


def _select_gather(keys, lo_w, table, base, n_keep):
    T, S = keys.shape
    W = table.shape[1]
    L = SC_LANES
    per_worker = T // SC_WORKERS
    n_idx_rows = n_keep // SC_MAX_INDEX_ROW
    mesh = plsc.VectorSubcoreMesh(core_axis_name="c", subcore_axis_name="s")
    cp = pltpu.CompilerParams()
    if "needs_layout_passes" in pltpu.CompilerParams.__dataclass_fields__:
        cp = dataclasses.replace(cp, needs_layout_passes=False)

    seg = min(SC_SEG, S)
    n_seg = S // seg
    unroll = 8
    per_row = SC_MAX_INDEX_ROW // L

    @functools.partial(
        pl.kernel, mesh=mesh, compiler_params=cp,
        out_type=(jax.ShapeDtypeStruct((T, n_idx_rows, SC_MAX_INDEX_ROW), jnp.int32),
                  jax.ShapeDtypeStruct((T, n_keep, W), jnp.int32)),
        scratch_types=[pltpu.VMEM((2, S), jnp.int32),
                       pltpu.VMEM((per_worker,), jnp.int32),
                       pltpu.VMEM((S,), jnp.int32),
                       pltpu.VMEM((n_idx_rows, SC_MAX_INDEX_ROW), jnp.int32),
                       pltpu.VMEM((2, n_idx_rows, SC_MAX_INDEX_ROW), jnp.int32),
                       pltpu.VMEM((2, n_keep, W), jnp.int32),
                       pltpu.SemaphoreType.DMA((2,)),
                       pltpu.SemaphoreType.DMA((2,)),
                       pltpu.SemaphoreType.DMA((2,))],
        name="select_gather",
    )
    def body(keys_hbm, lo_hbm, table_hbm, idx_hbm, rows_hbm,
             krow, lo_v, cand, idx_s, idx_g, rows_v, key_sem, gat_sem, out_sem):
        wid = lax.axis_index("s") * SC_CORES + lax.axis_index("c")
        lane = lax.iota(jnp.int32, L)
        zero = jnp.zeros((L,), jnp.int32)

        def key_copy(r, k, buf):
            return pltpu.make_async_copy(keys_hbm.at[r, pl.ds(k * seg, seg)],
                                         krow.at[buf, pl.ds(k * seg, seg)], key_sem.at[buf])

        def gather_copy(h, buf):
            return pltpu.make_async_copy(
                table_hbm.at[idx_g.at[buf, h]],
                rows_v.at[buf, pl.ds(h * SC_MAX_INDEX_ROW, SC_MAX_INDEX_ROW)], gat_sem.at[buf])

        def out_copies(r, buf):
            return (pltpu.make_async_copy(rows_v.at[buf], rows_hbm.at[r], out_sem.at[buf]),
                    pltpu.make_async_copy(idx_g.at[buf], idx_hbm.at[r], out_sem.at[buf]))

        def row_of(q):
            r = wid + SC_WORKERS * q
            return r, r, base

        def fetch_keys(q, buf):
            r, t, _ = row_of(q)
            for k in range(n_seg):
                @pl.when((q < per_worker) & (t >= k * seg))
                def _():
                    key_copy(r, k, buf).start()

        def step(q, buf):
            r, t, row0 = row_of(q)
            fetch_keys(q + 1, 1 - buf)
            for k in range(n_seg):
                @pl.when(t >= k * seg)
                def _():
                    key_copy(r, k, buf).wait()
            lo0 = plsc.load_gather(lo_v, [zero + q])
            fill = zero + row0
            for j in range(n_keep // L):
                idx_s[j // per_row, pl.ds((j % per_row) * L, L)] = fill

            def ones(m):
                return jnp.where(m, 1, 0).astype(jnp.int32)

            def pick(g, carry):
                cnt, top = carry
                vs = [krow[buf, pl.ds((g * unroll + u) * L, L)] for u in range(unroll)]
                for u, v in enumerate(vs):
                    m = v >= lo0
                    pos = cnt + plsc.cumsum(ones(m)) - 1
                    plsc.store_scatter(cand, [pos], (g * unroll + u) * L + lane, mask=m)
                    cnt = cnt + plsc.all_reduce_population_count(m)
                    top = jnp.maximum(top, v)
                return cnt, top

            n_cand, top = plsc.parallel_loop(0, t // (unroll * L) + 1,
                                             carry=(zero, zero + INT_MIN))(pick)
            n_vec = (jnp.max(n_cand) + L - 1) // L

            def pack(j, _):
                ok = j * L + lane < n_cand
                pos = jnp.where(ok, cand[pl.ds(j * L, L)], 0)
                krow[buf, pl.ds(j * L, L)] = plsc.load_gather(krow, [zero + buf, pos])
                return 0

            lax.fori_loop(0, n_vec, pack, 0)

            def cand_keys(j):
                ok = j * L + lane < n_cand
                return krow[buf, pl.ds(j * L, L)], cand[pl.ds(j * L, L)], ok

            count_unroll = 4

            def count(pred):
                def some(g, acc):
                    for u in range(count_unroll):
                        kv, _, ok = cand_keys(g * count_unroll + u)
                        acc = acc + plsc.all_reduce_population_count(pred(kv) & ok)
                    return acc
                return plsc.parallel_loop(0, (n_vec + count_unroll - 1) // count_unroll, carry=zero)(some)

            search = n_cand > n_keep

            def active(lo, hi, cnt):
                return search & (cnt != n_keep) & (lo + 1 < hi)

            def flip(v):
                return v ^ (lax.shift_right_arithmetic(v, 31) & jnp.int32(0x7FFFFFFF))

            def probe(lo, hi, step):
                mid = (lo & hi) + lax.shift_right_arithmetic(lo ^ hi, 1)
                mean = (0.5 * lax.bitcast_convert_type(flip(lo), jnp.float32)
                        + 0.5 * lax.bitcast_convert_type(flip(hi), jnp.float32))
                mid_v = flip(lax.bitcast_convert_type(mean, jnp.int32))
                width = hi - lo
                far = (width < 0) | (width > (1 << 24))
                mid = jnp.where(far & (step < SC_VALUE_STEPS) & (mid_v > lo) & (mid_v < hi), mid_v, mid)
                mid = jnp.where((lo < 0) & (hi > 0), 0, mid)
                return jnp.where((lo == 0) & (hi > 1), 1, mid)

            def search_body(carry):
                lo, hi, cnt, step, _ = carry
                mid = probe(lo, hi, step)
                n = count(lambda kv: kv >= mid)
                up = n >= n_keep
                lo, hi, cnt = jnp.where(up, mid, lo), jnp.where(up, hi, mid), jnp.where(up, n, cnt)
                return lo, hi, cnt, step + 1, jnp.max(ones(active(lo, hi, cnt)))

            hi0 = zero + jnp.max(top) + 1
            lo, hi, cnt, _, _ = lax.while_loop(
                lambda carry: carry[4] > 0, search_body,
                (lo0, hi0, n_cand, jnp.int32(0), jnp.max(ones(active(lo0, hi0, n_cand)))))

            tie = search & (cnt != n_keep)
            thr = jnp.where(tie, lo, lo - 1)
            n_eq = jnp.where(tie, n_keep - count(lambda kv: kv > lo), 0)

            def emit(j, carry):
                cnt, seen = carry
                kv, pos, ok = cand_keys(j)
                eq = (kv == thr) & ok
                rank = seen + plsc.cumsum(ones(eq))
                m = ok & ((kv > thr) | (eq & (rank <= n_eq)))
                slot = cnt + plsc.cumsum(ones(m)) - 1
                fits = m & (slot < n_keep)
                slot = jnp.minimum(slot, n_keep - 1)
                plsc.store_scatter(idx_s, [slot >> 7, slot & 127], pos + row0, mask=fits)
                return (cnt + plsc.all_reduce_population_count(m),
                        seen + plsc.all_reduce_population_count(eq))

            lax.fori_loop(0, n_vec, emit, (zero, zero))

            @pl.when(q >= 2)
            def _():
                for c in out_copies(r, buf):
                    c.wait()

            for j in range(n_keep // L):
                sl = (j // per_row, pl.ds((j % per_row) * L, L))
                idx_g[(buf,) + sl] = idx_s[sl]

            @pl.when(q >= 1)
            def _():
                for h in range(n_idx_rows):
                    gather_copy(h, 1 - buf).wait()
                for c in out_copies(r - SC_WORKERS, 1 - buf):
                    c.start()

            for h in range(n_idx_rows):
                gather_copy(h, buf).start()

        pltpu.sync_copy(lo_hbm.at[wid], lo_v)
        fetch_keys(jnp.int32(0), 0)

        def pair(p, _):
            step(2 * p, 0)
            step(2 * p + 1, 1)
            return 0

        lax.fori_loop(0, per_worker // 2, pair, 0)

        r_last = row_of(per_worker - 1)[0]
        for h in range(n_idx_rows):
            gather_copy(h, 1).wait()
        for c in out_copies(r_last, 1):
            c.start()
        for c in out_copies(r_last, 0) + out_copies(r_last, 1):
            c.wait()

    return body(keys, lo_w, table)


def _attn_kernel(ql_ref, kv_ref, idx_ref, tbl_ref, o_ref, kvb_ref, lg_ref, bk_ref, *, base):
    tq = ql_ref.shape[0]
    n_keep = kv_ref.shape[1]
    t0 = pl.program_id(0) * tq
    row0 = base
    max_exact = N_BUCKETS // 2

    t_row = t0 + lax.broadcasted_iota(jnp.int32, (tq, n_keep), 0)
    d = jnp.maximum(t_row - (idx_ref[...] - row0), 0)
    large = max_exact + (jnp.log(jnp.maximum(d, 1).astype(jnp.float32) / max_exact)
                         / math.log(MAX_DIST / max_exact) * (N_BUCKETS - max_exact)).astype(jnp.int32)
    bk_ref[...] = jnp.where(d < max_exact, d, jnp.minimum(large, N_BUCKETS - 1))
    bucket_id = lax.broadcasted_iota(jnp.int32, (N_BUCKETS, n_keep), 0)

    def logits_of(q, _):
        kv = _unpack_kv(kv_ref[q])
        kvb_ref[q] = kv
        onehot = jnp.where(bk_ref[pl.ds(q, 1), :] == bucket_id, 1.0, 0.0).astype(jnp.bfloat16)
        bias2 = jnp.dot(tbl_ref[...], onehot, preferred_element_type=jnp.float32)
        ql = ql_ref[q].astype(jnp.bfloat16)
        lg_ref[q] = (lax.dot_general(ql, kv, (((1,), (1,)), ((), ())), preferred_element_type=jnp.float32)
                     + bias2[:N_HEADS] + bias2[N_HEADS:])
        return 0

    lax.fori_loop(0, tq, logits_of, 0, unroll=4)

    shape = (tq, N_HEADS, n_keep)
    valid = lax.broadcasted_iota(jnp.int32, shape, 2) <= t0 + lax.broadcasted_iota(jnp.int32, shape, 0)
    logits = jnp.where(valid, lg_ref[...], -1e30)
    e = jnp.exp(logits - jnp.max(logits, axis=-1, keepdims=True))
    lg_ref[...] = e / jnp.sum(e, axis=-1, keepdims=True)

    def values_of(q, _):
        o_ref[q] = jnp.dot(lg_ref[q].astype(jnp.bfloat16), kvb_ref[q], preferred_element_type=jnp.float32)
        return 0

    lax.fori_loop(0, tq, values_of, 0, unroll=4)


def _attn(qlat, rows, idx, bias_tab, b, tq=32):
    T, n_keep, W = rows.shape
    nblk = T // tq
    return pl.pallas_call(
        functools.partial(_attn_kernel, base=b * T),
        grid=(nblk,),
        in_specs=[pl.BlockSpec((tq, N_HEADS, KV_RANK), lambda i: (b * nblk + i, 0, 0)),
                  pl.BlockSpec((tq, n_keep, W), lambda i: (i, 0, 0)),
                  pl.BlockSpec((tq, n_keep), lambda i: (i, 0)),
                  pl.BlockSpec((2 * N_HEADS, N_BUCKETS), lambda i: (0, 0))],
        out_specs=pl.BlockSpec((tq, N_HEADS, KV_RANK), lambda i: (i, 0, 0)),
        out_shape=jax.ShapeDtypeStruct((T, N_HEADS, KV_RANK), jnp.float32),
        scratch_shapes=[pltpu.VMEM((tq, n_keep, KV_RANK), jnp.bfloat16),
                        pltpu.VMEM((tq, N_HEADS, n_keep), jnp.float32),
                        pltpu.VMEM((tq, n_keep), jnp.int32)],
        compiler_params=_cparams("parallel"),
        name="attn",
    )(qlat, rows, idx, bias_tab)


def _oproj_kernel(o_ref, w_ref, y_ref):
    for h in range(N_HEADS):
        y_ref[:, h * HEAD_DIM:(h + 1) * HEAD_DIM] = jnp.dot(
            o_ref[:, h, :].astype(jnp.bfloat16), w_ref[h],
            preferred_element_type=jnp.float32).astype(y_ref.dtype)


def _oproj(o_lat, w_uv, tm=256):
    T = o_lat.shape[0]
    return pl.pallas_call(
        _oproj_kernel,
        grid=(T // tm,),
        in_specs=[pl.BlockSpec((tm, N_HEADS, KV_RANK), lambda i: (i, 0, 0)),
                  pl.BlockSpec((N_HEADS, KV_RANK, HEAD_DIM), lambda i: (0, 0, 0))],
        out_specs=pl.BlockSpec((tm, D_ATT), lambda i: (i, 0)),
        out_shape=jax.ShapeDtypeStruct((T, D_ATT), jnp.bfloat16),
        compiler_params=_cparams("parallel"),
        name="oproj",
    )(o_lat, w_uv)


def _merge_kernel(x_ref, yr_ref, ya_ref, gr_ref, ga_ref, pr_ref, pa_ref, wo_ref, g_ref, b_ref,
                  o_ref, *, alpha):
    a = jnp.dot(yr_ref[...], pr_ref[...], preferred_element_type=jnp.float32)
    c = jnp.dot(ya_ref[...], pa_ref[...], preferred_element_type=jnp.float32)
    merged = jax.nn.sigmoid(gr_ref[...]) * a + jax.nn.sigmoid(ga_ref[...]) * c
    mix = jnp.dot(merged.astype(jnp.bfloat16), wo_ref[...], preferred_element_type=jnp.float32)
    o_ref[...] = _layer_norm(alpha * x_ref[...] + mix, g_ref[...], b_ref[...])


def _merge(x, y_rnn, y_att, h_all, proj_rnn, proj_att, w_out, g, b, alpha, tm=512):
    T = x.shape[0]
    tok = lambda c: pl.BlockSpec((tm, D_MODEL), lambda i: (i, c))
    wgt = pl.BlockSpec((D_MODEL, D_MODEL), lambda i: (0, 0))
    vec = pl.BlockSpec((1, D_MODEL), lambda i: (0, 0))
    return pl.pallas_call(
        functools.partial(_merge_kernel, alpha=alpha),
        grid=(T // tm,),
        in_specs=[tok(0), tok(0), tok(0), tok(COL_GATE_RNN // D_MODEL), tok(COL_GATE_ATT // D_MODEL),
                  wgt, wgt, wgt, vec, vec],
        out_specs=tok(0),
        out_shape=jax.ShapeDtypeStruct((T, D_MODEL), jnp.float32),
        compiler_params=_cparams("parallel"),
        name="merge",
    )(x, y_rnn, y_att, h_all, h_all, proj_rnn, proj_att, w_out, g, b)


def _router_kernel(x_ref, w_ref, b_ref, o_ref):
    logits = lax.dot_general(w_ref[...], x_ref[...], (((1,), (1,)), ((), ())),
                             precision=lax.Precision.HIGHEST,
                             preferred_element_type=jnp.float32)
    aff = [jax.nn.sigmoid(logits[e:e + 1, :]) for e in range(N_EXPERTS)]
    sel = [aff[e] + b_ref[e:e + 1, :] for e in range(N_EXPERTS)]
    P = EXPERTS_PER_GROUP
    gscore = []
    for g in range(N_GROUPS):
        v = sel[g * P:(g + 1) * P]
        best = None
        for a in range(P):
            for c in range(a + 1, P):
                s2 = v[a] + v[c]
                best = s2 if best is None else jnp.maximum(best, s2)
        gscore.append(best)
    gbest = jnp.zeros_like(gscore[0], dtype=jnp.int32)
    gmax = gscore[0]
    for g in range(1, N_GROUPS):
        better = gscore[g] > gmax
        gbest = jnp.where(better, g, gbest)
        gmax = jnp.where(better, gscore[g], gmax)

    def pick(vals):
        out = vals[0]
        for g in range(1, N_GROUPS):
            out = jnp.where(gbest == g, vals[g], out)
        return out

    sv = [pick([sel[g * P + j] for g in range(N_GROUPS)]) for j in range(P)]
    av = [pick([aff[g * P + j] for g in range(N_GROUPS)]) for j in range(P)]

    def first_max(vals, skip=None):
        idx = None
        best = None
        for j in range(P):
            v = vals[j] if skip is None else jnp.where(skip == j, -jnp.inf, vals[j])
            if best is None:
                idx, best = jnp.zeros_like(gbest), v
            else:
                better = v > best
                idx = jnp.where(better, j, idx)
                best = jnp.where(better, v, best)
        return idx

    j1 = first_max(sv)
    j2 = first_max(sv, skip=j1)
    g1 = av[0]
    g2 = av[0]
    for j in range(1, P):
        g1 = jnp.where(j1 == j, av[j], g1)
        g2 = jnp.where(j2 == j, av[j], g2)
    den = g1 + g2
    rows = []
    for e in range(N_EXPERTS):
        g, j = divmod(e, P)
        w = jnp.where(j1 == j, g1 / den, jnp.where(j2 == j, g2 / den, 0.0))
        rows.append(jnp.where(gbest == g, w, 0.0))
    o_ref[...] = jnp.concatenate(rows, axis=0)


def _router(x, w_routerT, bias, tm=512):
    T = x.shape[0]
    return pl.pallas_call(
        _router_kernel,
        grid=(T // tm,),
        in_specs=[pl.BlockSpec((tm, D_MODEL), lambda i: (i, 0)),
                  pl.BlockSpec((N_EXPERTS, D_MODEL), lambda i: (0, 0)),
                  pl.BlockSpec((N_EXPERTS, 1), lambda i: (0, 0))],
        out_specs=pl.BlockSpec((N_EXPERTS, tm), lambda i: (0, i)),
        out_shape=jax.ShapeDtypeStruct((N_EXPERTS, T), jnp.float32),
        compiler_params=_cparams("parallel"),
        name="router",
    )(x, w_routerT, bias)


def _moe_kernel(x_ref, c_ref, wg_ref, wu_ref, wd_ref, g_ref, b_ref, o_ref, xb_ref, acc_ref, *, alpha):
    e = pl.program_id(1)

    @pl.when(e == 0)
    def _():
        xb_ref[...] = x_ref[...].astype(jnp.bfloat16)
        acc_ref[...] = jnp.zeros_like(acc_ref)

    xb = xb_ref[...]
    gate = jnp.dot(xb, wg_ref[0], preferred_element_type=jnp.float32)
    up = jnp.dot(xb, wu_ref[0], preferred_element_type=jnp.float32)
    he = (jax.nn.silu(gate) * up).astype(jnp.bfloat16)
    dn = jnp.dot(he, wd_ref[0], preferred_element_type=jnp.float32)
    lane = lax.broadcasted_iota(jnp.int32, c_ref.shape, 1)
    ce = jnp.sum(jnp.where(lane == e, c_ref[...], 0.0), axis=1, keepdims=True)
    acc_ref[...] += ce * dn

    @pl.when(e == N_EXPERTS - 1)
    def _():
        o_ref[...] = _layer_norm(alpha * x_ref[...] + acc_ref[...], g_ref[...], b_ref[...])


def _moe(x, comb, w_gate, w_up, w_down, g, b, alpha, tm=512):
    T = x.shape[0]
    vec = pl.BlockSpec((1, D_MODEL), lambda i, e: (0, 0))
    return pl.pallas_call(
        functools.partial(_moe_kernel, alpha=alpha),
        grid=(T // tm, N_EXPERTS),
        in_specs=[pl.BlockSpec((tm, D_MODEL), lambda i, e: (i, 0)),
                  pl.BlockSpec((tm, N_EXPERTS), lambda i, e: (i, 0)),
                  pl.BlockSpec((1, D_MODEL, D_EXPERT), lambda i, e: (e, 0, 0)),
                  pl.BlockSpec((1, D_MODEL, D_EXPERT), lambda i, e: (e, 0, 0)),
                  pl.BlockSpec((1, D_EXPERT, D_MODEL), lambda i, e: (e, 0, 0)),
                  vec, vec],
        out_specs=pl.BlockSpec((tm, D_MODEL), lambda i, e: (i, 0)),
        out_shape=jax.ShapeDtypeStruct((T, D_MODEL), jnp.float32),
        scratch_shapes=[pltpu.VMEM((tm, D_MODEL), jnp.bfloat16),
                        pltpu.VMEM((tm, D_MODEL), jnp.float32)],
        compiler_params=_cparams("parallel", "arbitrary"),
        name="moe",
    )(x, comb, w_gate, w_up, w_down, g, b)


def _pack_w_in(w):
    xr, gr, q, ckv, qi, ki, wi, gate_r, gate_a = jnp.split(
        w, [1024, 2048, 3072, 3328, 3840, 3904, 3912, 4936], axis=1)
    pad = jnp.zeros((w.shape[0], D_IN_PAD - w.shape[1]), w.dtype)
    return jnp.concatenate([xr, gr, q, gate_r, gate_a, qi, ckv, ki, wi, pad], axis=1).astype(jnp.bfloat16)


def _block_diag_tiles(w):
    per = 256 // RNN_BW
    w = w.reshape(D_RNN // 256, per, RNN_BW, RNN_BW)
    eye = jnp.eye(per, dtype=w.dtype)
    return jnp.einsum('cgij,gh->cgihj', w, eye).reshape(D_RNN // 256, 256, 256).astype(jnp.bfloat16)


def kernel(x, w_in, conv_w, conv_b, lru_wa, lru_ba, lru_wx, lru_bx, lru_lambda, kv_norm, w_uk, w_uv, proj_rnn, proj_att, w_out, ln1_g, ln1_b, w_router, router_bias, exp_w_gate, exp_w_up, exp_w_down, ln2_g, ln2_b, rel_bias):
    B, S, D = x.shape
    T = B * S
    depth = w_in.shape[0]
    alpha = (2 * depth) ** 0.25
    n_keep = min(TOPK_MAX, S // 4)
    bf = jnp.bfloat16
    row = lambda v: v.reshape(1, -1)

    bias_hi = rel_bias.T.astype(bf)
    bias_lo = (rel_bias.T - bias_hi.astype(jnp.float32)).astype(bf)
    bias_tab = jnp.concatenate([bias_hi, bias_lo], axis=0)
    w_routerT = w_router.T
    rbias = router_bias.reshape(N_EXPERTS, 1)

    x = x.reshape(T, D)
    for l in range(depth):
        h_all = _inproj(x, _pack_w_in(w_in[l]))
        y_rnn = _rglru(h_all, B, S, conv_w[l], row(conv_b[l]), _block_diag_tiles(lru_wa[l]), row(lru_ba[l]),
                       _block_diag_tiles(lru_wx[l]), row(lru_bx[l]), row(lru_lambda[l]))
        table = _kvpack(h_all, row(kv_norm[l]))
        qlat = _qlat(h_all, jnp.swapaxes(w_uk[l], 1, 2).astype(bf))
        kidx = h_all[:, COL_KIDX:COL_KIDX + IDX_DIM].astype(bf).reshape(B, S, IDX_DIM)
        gathered = []
        for b in range(B):
            keys, lo = _indexer(h_all, kidx, b, S)
            lo_w = lo[0, :, 0].reshape(S // SC_WORKERS, SC_WORKERS).T
            gathered.append(_select_gather(keys[0], lo_w, table, b * S, n_keep))
        w_uv_l = w_uv[l].astype(bf)
        y_att = jnp.concatenate(
            [_oproj(_attn(qlat, rows, idx.reshape(S, n_keep), bias_tab, b), w_uv_l)
             for b, (idx, rows) in enumerate(gathered)], axis=0)
        x = _merge(x, y_rnn, y_att, h_all, proj_rnn[l].astype(bf), proj_att[l].astype(bf),
                   w_out[l].astype(bf), row(ln1_g[l]), row(ln1_b[l]), alpha)
        comb = _router(x, w_routerT, rbias).T
        x = _moe(x, comb, exp_w_gate[l].astype(bf), exp_w_up[l].astype(bf), exp_w_down[l].astype(bf),
                 row(ln2_g[l]), row(ln2_b[l]), alpha)
    return x.reshape(B, S, D)
```

```python
import dataclasses
import functools
import math

import jax
import jax.numpy as jnp
from jax import lax
from jax.experimental import pallas as pl
from jax.experimental.pallas import tpu as pltpu
from jax.experimental.pallas import tpu_sc as plsc

D_MODEL = 1024
D_RNN = 1024
RNN_BLOCKS = 16
RNN_BW = D_RNN // RNN_BLOCKS
CONV_W = 4
LRU_C = 8.0
N_HEADS = 8
HEAD_DIM = 128
D_ATT = N_HEADS * HEAD_DIM
KV_RANK = 256
IDX_HEADS = 8
IDX_DIM = 64
TOPK_MAX = 256
N_BUCKETS = 32
MAX_DIST = 128
N_EXPERTS = 16
N_GROUPS = 4
EXPERTS_PER_GROUP = N_EXPERTS // N_GROUPS
D_EXPERT = 512
LN_EPS = 1e-5
RMS_EPS = 1e-6

COL_XRNN = 0
COL_GRNN = 1024
COL_Q = 2048
COL_GATE_RNN = 3072
COL_GATE_ATT = 4096
COL_QIDX = 5120
COL_CKV = 5632
COL_KIDX = 5888
D_IN_PAD = 6144

SC_CORES = 2
SC_SUBCORES = 16
SC_LANES = 16
SC_WORKERS = SC_CORES * SC_SUBCORES
SC_MAX_INDEX_ROW = 128

INT_MIN = -2 ** 31
VMEM_LIMIT = 56 * 1024 * 1024


def _cparams(*sem):
    return pltpu.CompilerParams(dimension_semantics=sem, vmem_limit_bytes=VMEM_LIMIT)


def _layer_norm(v, g, b):
    mu = jnp.mean(v, axis=-1, keepdims=True)
    var = jnp.mean(jnp.square(v - mu), axis=-1, keepdims=True)
    return (v - mu) * lax.rsqrt(var + LN_EPS) * g + b


def _inproj_kernel(x_ref, w_ref, o_ref, xb_ref):
    @pl.when(pl.program_id(1) == 0)
    def _():
        xb_ref[...] = x_ref[...].astype(jnp.bfloat16)

    o_ref[...] = jnp.dot(xb_ref[...], w_ref[...], preferred_element_type=jnp.float32)


def _inproj(x, w, tm=1024, tn=512):
    T, K = x.shape
    N = w.shape[1]
    return pl.pallas_call(
        _inproj_kernel,
        grid=(T // tm, N // tn),
        in_specs=[pl.BlockSpec((tm, K), lambda i, j: (i, 0)),
                  pl.BlockSpec((K, tn), lambda i, j: (0, j))],
        out_specs=pl.BlockSpec((tm, tn), lambda i, j: (i, j)),
        out_shape=jax.ShapeDtypeStruct((T, N), jnp.float32),
        scratch_shapes=[pltpu.VMEM((tm, K), jnp.bfloat16)],
        compiler_params=_cparams("parallel", "arbitrary"),
        name="inproj",
    )(x, w)


def _rglru_kernel(x_ref, g_ref, cw_ref, cb_ref, wa_ref, ba_ref, wx_ref, bx_ref, lam_ref,
                  o_ref, prev_ref, carry_ref, a_ref, u_ref):
    ts = x_ref.shape[0]

    @pl.when(pl.program_id(1) == 0)
    def _():
        prev_ref[...] = jnp.zeros_like(prev_ref)
        carry_ref[...] = jnp.zeros_like(carry_ref)

    x = x_ref[...]
    xe = jnp.concatenate([prev_ref[...], x], axis=0)
    xr = cb_ref[...] + sum(cw_ref[k:k + 1, :] * xe[5 + k:5 + k + ts, :] for k in range(CONV_W))
    prev_ref[...] = x[ts - 8:, :]

    xb = xr.astype(jnp.bfloat16)
    nt = D_RNN // 256
    ra = jnp.concatenate([jnp.dot(xb[:, c * 256:(c + 1) * 256], wa_ref[c],
                                  preferred_element_type=jnp.float32) for c in range(nt)], axis=1)
    rx = jnp.concatenate([jnp.dot(xb[:, c * 256:(c + 1) * 256], wx_ref[c],
                                  preferred_element_type=jnp.float32) for c in range(nt)], axis=1)
    r = jax.nn.sigmoid(ra + ba_ref[...])
    gi = jax.nn.sigmoid(rx + bx_ref[...])
    z = -lam_ref[...]
    softplus = jnp.maximum(z, 0.0) + jnp.log(1.0 + jnp.exp(-jnp.abs(z)))
    log_a = (-LRU_C * r) * softplus
    a_ref[...] = jnp.exp(log_a)
    u_ref[...] = jnp.sqrt(1.0 - jnp.exp(2.0 * log_a)) * (gi * xr)

    row = lax.broadcasted_iota(jnp.int32, (8, D_RNN), 0)

    def group(gidx, carry):
        r0 = pl.multiple_of(gidx * 8, 8)
        a8 = a_ref[pl.ds(r0, 8), :]
        u8 = u_ref[pl.ds(r0, 8), :]
        for d in (1, 2, 4):
            keep = row >= d
            a_sh = pltpu.roll(a8, d, 0)
            u_sh = pltpu.roll(u8, d, 0)
            u8 = jnp.where(keep, a8 * u_sh + u8, u8)
            a8 = jnp.where(keep, a8 * a_sh, a8)
        h8 = a8 * carry + u8
        u_ref[pl.ds(r0, 8), :] = h8
        return h8[7:8, :]

    carry_ref[...] = lax.fori_loop(0, ts // 8, group, carry_ref[...], unroll=4)
    o_ref[...] = (u_ref[...] * jax.nn.gelu(g_ref[...])).astype(o_ref.dtype)


def _rglru(h_all, B, S, cw, cb, wa, ba, wx, bx, lam, ts=256):
    nblk = S // ts
    row = lambda b, i: b * nblk + i
    vec = pl.BlockSpec((1, D_RNN), lambda b, i: (0, 0))
    tile = pl.BlockSpec((D_RNN // 256, 256, 256), lambda b, i: (0, 0, 0))
    return pl.pallas_call(
        _rglru_kernel,
        grid=(B, nblk),
        in_specs=[pl.BlockSpec((ts, D_RNN), lambda b, i: (row(b, i), COL_XRNN // D_RNN)),
                  pl.BlockSpec((ts, D_RNN), lambda b, i: (row(b, i), COL_GRNN // D_RNN)),
                  pl.BlockSpec((CONV_W, D_RNN), lambda b, i: (0, 0)),
                  vec, tile, vec, tile, vec, vec],
        out_specs=pl.BlockSpec((ts, D_RNN), lambda b, i: (row(b, i), 0)),
        out_shape=jax.ShapeDtypeStruct((B * S, D_RNN), jnp.bfloat16),
        scratch_shapes=[pltpu.VMEM((8, D_RNN), jnp.float32),
                        pltpu.VMEM((1, D_RNN), jnp.float32),
                        pltpu.VMEM((ts, D_RNN), jnp.float32),
                        pltpu.VMEM((ts, D_RNN), jnp.float32)],
        compiler_params=_cparams("arbitrary", "arbitrary"),
        name="rglru",
    )(h_all, h_all, cw, cb, wa, ba, wx, bx, lam)


def _kvpack_kernel(c_ref, g_ref, o_ref):
    c = c_ref[...]
    cn = c * lax.rsqrt(jnp.mean(jnp.square(c), axis=-1, keepdims=True) + RMS_EPS) * g_ref[...]
    cb = cn.astype(jnp.bfloat16).astype(jnp.float32)
    half = KV_RANK // 2
    lo = lax.bitcast_convert_type(cb[:, :half], jnp.int32)
    hi = lax.bitcast_convert_type(cb[:, half:], jnp.int32)
    o_ref[...] = (hi & jnp.int32(-65536)) | lax.shift_right_logical(lo, 16)


def _kvpack(h_all, kv_norm, ts=1024):
    T = h_all.shape[0]
    return pl.pallas_call(
        _kvpack_kernel,
        grid=(T // ts,),
        in_specs=[pl.BlockSpec((ts, KV_RANK), lambda i: (i, COL_CKV // KV_RANK)),
                  pl.BlockSpec((1, KV_RANK), lambda i: (0, 0))],
        out_specs=pl.BlockSpec((ts, KV_RANK // 2), lambda i: (i, 0)),
        out_shape=jax.ShapeDtypeStruct((T, KV_RANK // 2), jnp.int32),
        compiler_params=_cparams("parallel"),
        name="kvpack",
    )(h_all, kv_norm)


def _unpack_kv(w):
    lo = lax.bitcast_convert_type(lax.shift_left(w, 16), jnp.float32)
    hi = lax.bitcast_convert_type(w & jnp.int32(-65536), jnp.float32)
    return jnp.concatenate([lo, hi], axis=-1).astype(jnp.bfloat16)


def _qlat_kernel(q_ref, w_ref, o_ref):
    q = q_ref[...].astype(jnp.bfloat16)
    for h in range(N_HEADS):
        o_ref[:, h, :] = jnp.dot(q[:, h * HEAD_DIM:(h + 1) * HEAD_DIM], w_ref[h],
                                 preferred_element_type=jnp.float32) * (HEAD_DIM ** -0.5)


def _qlat(h_all, w_ukT, tm=256):
    T = h_all.shape[0]
    return pl.pallas_call(
        _qlat_kernel,
        grid=(T // tm,),
        in_specs=[pl.BlockSpec((tm, D_ATT), lambda i: (i, COL_Q // D_ATT)),
                  pl.BlockSpec((N_HEADS, HEAD_DIM, KV_RANK), lambda i: (0, 0, 0))],
        out_specs=pl.BlockSpec((tm, N_HEADS, KV_RANK), lambda i: (i, 0, 0)),
        out_shape=jax.ShapeDtypeStruct((T, N_HEADS, KV_RANK), jnp.float32),
        compiler_params=_cparams("parallel"),
        name="qlat",
    )(h_all, w_ukT)


IDX_QB = 128
IDX_KC = 512
IDX_TOPS = 3


def _indexer_kernel(q_ref, kw_ref, k_ref, keys_ref, lo_ref, *, n_keep):
    S = keys_ref.shape[2]
    i = pl.program_id(0)
    n_causal = (i * IDX_QB + IDX_QB - 1) // IDX_KC + 1
    q_all = jnp.concatenate(
        [(q_ref[:, h * IDX_DIM:(h + 1) * IDX_DIM] * (IDX_DIM ** -0.5)).astype(jnp.bfloat16)
         for h in range(IDX_HEADS)], axis=0)
    ws = [kw_ref[:, IDX_DIM + h:IDX_DIM + h + 1] * (IDX_HEADS ** -0.5) for h in range(IDX_HEADS)]
    t = i * IDX_QB + lax.broadcasted_iota(jnp.int32, (IDX_QB, IDX_KC), 0)
    lane = lax.broadcasted_iota(jnp.int32, (IDX_QB, IDX_KC), 1)
    n_slab = IDX_KC // 128

    def score_chunk(c, carry):
        tops = list(carry)
        c0 = pl.multiple_of(c * IDX_KC, IDX_KC)
        kc = k_ref[0, pl.ds(c0, IDX_KC), :]
        d = lax.dot_general(q_all, kc, (((1,), (1,)), ((), ())), preferred_element_type=jnp.float32)
        sc = jnp.zeros((IDX_QB, IDX_KC), jnp.float32)
        for h in range(IDX_HEADS):
            sc = sc + jnp.maximum(d[h * IDX_QB:(h + 1) * IDX_QB], 0.0) * ws[h]
        bits = lax.bitcast_convert_type(sc, jnp.int32)
        key = bits ^ (lax.shift_right_arithmetic(bits, 31) & jnp.int32(0x7FFFFFFF))
        key = jnp.where(c0 + lane <= t, key, jnp.int32(INT_MIN))
        keys_ref[0, :, pl.ds(c0, IDX_KC)] = key
        for j in range(n_slab):
            x = key[:, j * 128:(j + 1) * 128]
            for r in range(IDX_TOPS - 1, 0, -1):
                tops[r] = jnp.maximum(tops[r], jnp.minimum(tops[r - 1], x))
            tops[0] = jnp.maximum(tops[0], x)
        return tuple(tops)

    floor = jnp.full((IDX_QB, 128), INT_MIN, jnp.int32)
    tops = lax.fori_loop(0, n_causal, score_chunk, (floor,) * IDX_TOPS)

    def blank_chunk(c, _):
        c0 = pl.multiple_of(c * IDX_KC, IDX_KC)
        keys_ref[0, :, pl.ds(c0, IDX_KC)] = jnp.full((IDX_QB, IDX_KC), INT_MIN, jnp.int32)
        return 0

    lax.fori_loop(n_causal, S // IDX_KC, blank_chunk, 0)

    def bit_pass(p, res):
        cand = res | lax.shift_left(jnp.int32(1), 31 - p)
        cand_b = jnp.broadcast_to(cand ^ jnp.int32(INT_MIN), (IDX_QB, 128))
        hits = sum(jnp.where(tr >= cand_b, 1, 0) for tr in tops)
        return jnp.where(jnp.sum(hits, axis=1, keepdims=True) >= n_keep, cand, res)

    res = lax.fori_loop(0, 32, bit_pass, jnp.zeros((IDX_QB, 1), jnp.int32))
    lo = jnp.maximum(res ^ jnp.int32(INT_MIN), jnp.int32(INT_MIN + 1))
    lo_ref[0] = jnp.broadcast_to(lo, (IDX_QB, 128))


def _indexer(h_all, kidx, b, S, n_keep):
    assert n_keep <= 128 * IDX_TOPS
    nblk = S // IDX_QB
    return pl.pallas_call(
        functools.partial(_indexer_kernel, n_keep=n_keep),
        grid=(nblk,),
        in_specs=[pl.BlockSpec((IDX_QB, IDX_HEADS * IDX_DIM), lambda i: (b * nblk + i, COL_QIDX // 512)),
                  pl.BlockSpec((IDX_QB, 128), lambda i: (b * nblk + i, COL_KIDX // 128)),
                  pl.BlockSpec((1, S, IDX_DIM), lambda i: (b, 0, 0))],
        out_specs=[pl.BlockSpec((1, IDX_QB, S), lambda i: (0, i, 0)),
                   pl.BlockSpec((1, IDX_QB, 128), lambda i: (0, i, 0))],
        out_shape=[jax.ShapeDtypeStruct((1, S, S), jnp.int32),
                   jax.ShapeDtypeStruct((1, S, 128), jnp.int32)],
        compiler_params=_cparams("parallel"),
        name="indexer",
    )(h_all, h_all, kidx)


SC_SEG = 4096
SC_VALUE_STEPS = 40


def _select_gather(keys, lo_w, table, base, n_keep):
    T, S = keys.shape
    W = table.shape[1]
    L = SC_LANES
    per_worker = T // SC_WORKERS
    n_idx_rows = n_keep // SC_MAX_INDEX_ROW
    mesh = plsc.VectorSubcoreMesh(core_axis_name="c", subcore_axis_name="s")
    cp = pltpu.CompilerParams()
    if "needs_layout_passes" in pltpu.CompilerParams.__dataclass_fields__:
        cp = dataclasses.replace(cp, needs_layout_passes=False)

    seg = min(SC_SEG, S)
    n_seg = S // seg
    unroll = 8
    per_row = SC_MAX_INDEX_ROW // L

    @functools.partial(
        pl.kernel, mesh=mesh, compiler_params=cp,
        out_type=(jax.ShapeDtypeStruct((T, n_idx_rows, SC_MAX_INDEX_ROW), jnp.int32),
                  jax.ShapeDtypeStruct((T, n_keep, W), jnp.int32)),
        scratch_types=[pltpu.VMEM((2, S), jnp.int32),
                       pltpu.VMEM((per_worker,), jnp.int32),
                       pltpu.VMEM((S,), jnp.int32),
                       pltpu.VMEM((n_idx_rows, SC_MAX_INDEX_ROW), jnp.int32),
                       pltpu.VMEM((2, n_idx_rows, SC_MAX_INDEX_ROW), jnp.int32),
                       pltpu.VMEM((2, n_keep, W), jnp.int32),
                       pltpu.SemaphoreType.DMA((2,)),
                       pltpu.SemaphoreType.DMA((2,)),
                       pltpu.SemaphoreType.DMA((2,))],
        name="select_gather",
    )
    def body(keys_hbm, lo_hbm, table_hbm, idx_hbm, rows_hbm,
             krow, lo_v, cand, idx_s, idx_g, rows_v, key_sem, gat_sem, out_sem):
        wid = lax.axis_index("s") * SC_CORES + lax.axis_index("c")
        lane = lax.iota(jnp.int32, L)
        zero = jnp.zeros((L,), jnp.int32)

        def key_copy(r, k, buf):
            return pltpu.make_async_copy(keys_hbm.at[r, pl.ds(k * seg, seg)],
                                         krow.at[buf, pl.ds(k * seg, seg)], key_sem.at[buf])

        def gather_copy(h, buf):
            return pltpu.make_async_copy(
                table_hbm.at[idx_g.at[buf, h]],
                rows_v.at[buf, pl.ds(h * SC_MAX_INDEX_ROW, SC_MAX_INDEX_ROW)], gat_sem.at[buf])

        def out_copies(r, buf):
            return (pltpu.make_async_copy(rows_v.at[buf], rows_hbm.at[r], out_sem.at[buf]),
                    pltpu.make_async_copy(idx_g.at[buf], idx_hbm.at[r], out_sem.at[buf]))

        def row_of(q):
            r = wid + SC_WORKERS * q
            return r, r, base

        def fetch_keys(q, buf):
            r, t, _ = row_of(q)
            for k in range(n_seg):
                @pl.when((q < per_worker) & (t >= k * seg))
                def _():
                    key_copy(r, k, buf).start()

        def step(q, buf):
            r, t, row0 = row_of(q)
            fetch_keys(q + 1, 1 - buf)
            for k in range(n_seg):
                @pl.when(t >= k * seg)
                def _():
                    key_copy(r, k, buf).wait()
            lo0 = plsc.load_gather(lo_v, [zero + q])
            fill = zero + row0
            for j in range(n_keep // L):
                idx_s[j // per_row, pl.ds((j % per_row) * L, L)] = fill

            def ones(m):
                return jnp.where(m, 1, 0).astype(jnp.int32)

            def pick(g, carry):
                cnt, top = carry
                vs = [krow[buf, pl.ds((g * unroll + u) * L, L)] for u in range(unroll)]
                for u, v in enumerate(vs):
                    m = v >= lo0
                    pos = cnt + plsc.cumsum(ones(m)) - 1
                    plsc.store_scatter(cand, [pos], (g * unroll + u) * L + lane, mask=m)
                    cnt = cnt + plsc.all_reduce_population_count(m)
                    top = jnp.maximum(top, v)
                return cnt, top

            n_cand, top = plsc.parallel_loop(0, t // (unroll * L) + 1,
                                             carry=(zero, zero + INT_MIN))(pick)
            n_vec = (jnp.max(n_cand) + L - 1) // L

            def pack(j, _):
                ok = j * L + lane < n_cand
                pos = jnp.where(ok, cand[pl.ds(j * L, L)], 0)
                krow[buf, pl.ds(j * L, L)] = plsc.load_gather(krow, [zero + buf, pos])
                return 0

            lax.fori_loop(0, n_vec, pack, 0)

            def cand_keys(j):
                ok = j * L + lane < n_cand
                return krow[buf, pl.ds(j * L, L)], cand[pl.ds(j * L, L)], ok

            count_unroll = 4

            def count(pred):
                def some(g, acc):
                    for u in range(count_unroll):
                        kv, _, ok = cand_keys(g * count_unroll + u)
                        acc = acc + plsc.all_reduce_population_count(pred(kv) & ok)
                    return acc
                return plsc.parallel_loop(0, (n_vec + count_unroll - 1) // count_unroll, carry=zero)(some)

            search = n_cand > n_keep

            def active(lo, hi, cnt):
                return search & (cnt != n_keep) & (lo + 1 < hi)

            def flip(v):
                return v ^ (lax.shift_right_arithmetic(v, 31) & jnp.int32(0x7FFFFFFF))

            def probe(lo, hi, step):
                mid = (lo & hi) + lax.shift_right_arithmetic(lo ^ hi, 1)
                mean = (0.5 * lax.bitcast_convert_type(flip(lo), jnp.float32)
                        + 0.5 * lax.bitcast_convert_type(flip(hi), jnp.float32))
                mid_v = flip(lax.bitcast_convert_type(mean, jnp.int32))
                width = hi - lo
                far = (width < 0) | (width > (1 << 24))
                mid = jnp.where(far & (step < SC_VALUE_STEPS) & (mid_v > lo) & (mid_v < hi), mid_v, mid)
                mid = jnp.where((lo < 0) & (hi > 0), 0, mid)
                return jnp.where((lo == 0) & (hi > 1), 1, mid)

            def search_body(carry):
                lo, hi, cnt, step, _ = carry
                mid = probe(lo, hi, step)
                n = count(lambda kv: kv >= mid)
                up = n >= n_keep
                lo, hi, cnt = jnp.where(up, mid, lo), jnp.where(up, hi, mid), jnp.where(up, n, cnt)
                return lo, hi, cnt, step + 1, jnp.max(ones(active(lo, hi, cnt)))

            hi0 = zero + jnp.max(top) + 1
            lo, hi, cnt, _, _ = lax.while_loop(
                lambda carry: carry[4] > 0, search_body,
                (lo0, hi0, n_cand, jnp.int32(0), jnp.max(ones(active(lo0, hi0, n_cand)))))

            tie = search & (cnt != n_keep)
            thr = jnp.where(tie, lo, lo - 1)
            n_eq = jnp.where(tie, n_keep - count(lambda kv: kv > lo), 0)

            def emit(j, carry):
                cnt, seen = carry
                kv, pos, ok = cand_keys(j)
                eq = (kv == thr) & ok
                rank = seen + plsc.cumsum(ones(eq))
                m = ok & ((kv > thr) | (eq & (rank <= n_eq)))
                slot = cnt + plsc.cumsum(ones(m)) - 1
                fits = m & (slot < n_keep)
                slot = jnp.minimum(slot, n_keep - 1)
                plsc.store_scatter(idx_s, [slot >> 7, slot & 127], pos + row0, mask=fits)
                return (cnt + plsc.all_reduce_population_count(m),
                        seen + plsc.all_reduce_population_count(eq))

            lax.fori_loop(0, n_vec, emit, (zero, zero))

            @pl.when(q >= 2)
            def _():
                for c in out_copies(r, buf):
                    c.wait()

            for j in range(n_keep // L):
                sl = (j // per_row, pl.ds((j % per_row) * L, L))
                idx_g[(buf,) + sl] = idx_s[sl]

            @pl.when(q >= 1)
            def _():
                for h in range(n_idx_rows):
                    gather_copy(h, 1 - buf).wait()
                for c in out_copies(r - SC_WORKERS, 1 - buf):
                    c.start()

            for h in range(n_idx_rows):
                gather_copy(h, buf).start()

        pltpu.sync_copy(lo_hbm.at[wid], lo_v)
        fetch_keys(jnp.int32(0), 0)

        def pair(p, _):
            step(2 * p, 0)
            step(2 * p + 1, 1)
            return 0

        lax.fori_loop(0, per_worker // 2, pair, 0)

        r_last = row_of(per_worker - 1)[0]
        for h in range(n_idx_rows):
            gather_copy(h, 1).wait()
        for c in out_copies(r_last, 1):
            c.start()
        for c in out_copies(r_last, 0) + out_copies(r_last, 1):
            c.wait()

    return body(keys, lo_w, table)


ATT_GROUP = 8


def _attn_kernel(ql_ref, kv_ref, idx_ref, tbl_ref, o_ref, kvb_ref, lg_ref, bk_ref, *, base):
    tq = ql_ref.shape[0]
    n_keep = kv_ref.shape[1]
    t0 = pl.program_id(0) * tq
    row0 = base
    max_exact = N_BUCKETS // 2

    t_row = t0 + lax.broadcasted_iota(jnp.int32, (tq, n_keep), 0)
    d = jnp.maximum(t_row - (idx_ref[...] - row0), 0)
    large = max_exact + (jnp.log(jnp.maximum(d, 1).astype(jnp.float32) / max_exact)
                         / math.log(MAX_DIST / max_exact) * (N_BUCKETS - max_exact)).astype(jnp.int32)
    bk_ref[...] = jnp.where(d < max_exact, d, jnp.minimum(large, N_BUCKETS - 1))
    bucket_id = lax.broadcasted_iota(jnp.int32, (N_BUCKETS, ATT_GROUP * n_keep), 0)

    G = ATT_GROUP

    def logits_of(g, _):
        q0 = pl.multiple_of(g * G, G)
        kv = _unpack_kv(kv_ref[pl.ds(q0, G)])
        kvb_ref[pl.ds(q0, G)] = kv
        buckets = jnp.concatenate([bk_ref[pl.ds(q0 + i, 1), :] for i in range(G)], axis=1)
        onehot = jnp.where(buckets == bucket_id, 1.0, 0.0).astype(jnp.bfloat16)
        bias2 = jnp.dot(tbl_ref[...], onehot, preferred_element_type=jnp.float32)
        bias = bias2[:N_HEADS] + bias2[N_HEADS:]
        ql = ql_ref[pl.ds(q0, G)].reshape(G * N_HEADS, KV_RANK).astype(jnp.bfloat16)
        full = lax.dot_general(ql, kv.reshape(G * n_keep, KV_RANK), (((1,), (1,)), ((), ())),
                               preferred_element_type=jnp.float32)
        for i in range(G):
            cols = slice(i * n_keep, (i + 1) * n_keep)
            lg_ref[q0 + i] = full[i * N_HEADS:(i + 1) * N_HEADS, cols] + bias[:, cols]
        return 0

    lax.fori_loop(0, tq // G, logits_of, 0, unroll=2)

    shape = (tq, N_HEADS, n_keep)
    valid = lax.broadcasted_iota(jnp.int32, shape, 2) <= t0 + lax.broadcasted_iota(jnp.int32, shape, 0)
    logits = jnp.where(valid, lg_ref[...], -1e30)
    e = jnp.exp(logits - jnp.max(logits, axis=-1, keepdims=True))
    lg_ref[...] = e / jnp.sum(e, axis=-1, keepdims=True)

    def values_of(g, _):
        q0 = pl.multiple_of(g * G, G)
        p = lg_ref[pl.ds(q0, G)]
        nothing = jnp.zeros((N_HEADS, n_keep), jnp.float32)
        p_diag = jnp.concatenate(
            [jnp.concatenate([p[i] if j == i else nothing for j in range(G)], axis=1) for i in range(G)],
            axis=0)
        out = jnp.dot(p_diag.astype(jnp.bfloat16), kvb_ref[pl.ds(q0, G)].reshape(G * n_keep, KV_RANK),
                      preferred_element_type=jnp.float32)
        o_ref[pl.ds(q0, G)] = out.reshape(G, N_HEADS, KV_RANK)
        return 0

    lax.fori_loop(0, tq // G, values_of, 0, unroll=2)


def _attn(qlat, rows, idx, bias_tab, b, tq=32):
    T, n_keep, W = rows.shape
    nblk = T // tq
    return pl.pallas_call(
        functools.partial(_attn_kernel, base=b * T),
        grid=(nblk,),
        in_specs=[pl.BlockSpec((tq, N_HEADS, KV_RANK), lambda i: (b * nblk + i, 0, 0)),
                  pl.BlockSpec((tq, n_keep, W), lambda i: (i, 0, 0)),
                  pl.BlockSpec((tq, n_keep), lambda i: (i, 0)),
                  pl.BlockSpec((2 * N_HEADS, N_BUCKETS), lambda i: (0, 0))],
        out_specs=pl.BlockSpec((tq, N_HEADS, KV_RANK), lambda i: (i, 0, 0)),
        out_shape=jax.ShapeDtypeStruct((T, N_HEADS, KV_RANK), jnp.float32),
        scratch_shapes=[pltpu.VMEM((tq, n_keep, KV_RANK), jnp.bfloat16),
                        pltpu.VMEM((tq, N_HEADS, n_keep), jnp.float32),
                        pltpu.VMEM((tq, n_keep), jnp.int32)],
        compiler_params=_cparams("parallel"),
        name="attn",
    )(qlat, rows, idx, bias_tab)


def _oproj_kernel(o_ref, w_ref, y_ref):
    for h in range(N_HEADS):
        y_ref[:, h * HEAD_DIM:(h + 1) * HEAD_DIM] = jnp.dot(
            o_ref[:, h, :].astype(jnp.bfloat16), w_ref[h],
            preferred_element_type=jnp.float32).astype(y_ref.dtype)


def _oproj(o_lat, w_uv, tm=256):
    T = o_lat.shape[0]
    return pl.pallas_call(
        _oproj_kernel,
        grid=(T // tm,),
        in_specs=[pl.BlockSpec((tm, N_HEADS, KV_RANK), lambda i: (i, 0, 0)),
                  pl.BlockSpec((N_HEADS, KV_RANK, HEAD_DIM), lambda i: (0, 0, 0))],
        out_specs=pl.BlockSpec((tm, D_ATT), lambda i: (i, 0)),
        out_shape=jax.ShapeDtypeStruct((T, D_ATT), jnp.bfloat16),
        compiler_params=_cparams("parallel"),
        name="oproj",
    )(o_lat, w_uv)


def _merge_kernel(x_ref, yr_ref, ya_ref, gr_ref, ga_ref, pr_ref, pa_ref, wo_ref, g_ref, b_ref,
                  o_ref, *, alpha):
    a = jnp.dot(yr_ref[...], pr_ref[...], preferred_element_type=jnp.float32)
    c = jnp.dot(ya_ref[...], pa_ref[...], preferred_element_type=jnp.float32)
    merged = jax.nn.sigmoid(gr_ref[...]) * a + jax.nn.sigmoid(ga_ref[...]) * c
    mix = jnp.dot(merged.astype(jnp.bfloat16), wo_ref[...], preferred_element_type=jnp.float32)
    o_ref[...] = _layer_norm(alpha * x_ref[...] + mix, g_ref[...], b_ref[...])


def _merge(x, y_rnn, y_att, h_all, proj_rnn, proj_att, w_out, g, b, alpha, tm=512):
    T = x.shape[0]
    tok = lambda c: pl.BlockSpec((tm, D_MODEL), lambda i: (i, c))
    wgt = pl.BlockSpec((D_MODEL, D_MODEL), lambda i: (0, 0))
    vec = pl.BlockSpec((1, D_MODEL), lambda i: (0, 0))
    return pl.pallas_call(
        functools.partial(_merge_kernel, alpha=alpha),
        grid=(T // tm,),
        in_specs=[tok(0), tok(0), tok(0), tok(COL_GATE_RNN // D_MODEL), tok(COL_GATE_ATT // D_MODEL),
                  wgt, wgt, wgt, vec, vec],
        out_specs=tok(0),
        out_shape=jax.ShapeDtypeStruct((T, D_MODEL), jnp.float32),
        compiler_params=_cparams("parallel"),
        name="merge",
    )(x, y_rnn, y_att, h_all, h_all, proj_rnn, proj_att, w_out, g, b)


def _router_kernel(x_ref, w_ref, b_ref, o_ref):
    logits = lax.dot_general(w_ref[...], x_ref[...], (((1,), (1,)), ((), ())),
                             precision=lax.Precision.HIGHEST,
                             preferred_element_type=jnp.float32)
    aff = [jax.nn.sigmoid(logits[e:e + 1, :]) for e in range(N_EXPERTS)]
    sel = [aff[e] + b_ref[e:e + 1, :] for e in range(N_EXPERTS)]
    P = EXPERTS_PER_GROUP
    gscore = []
    for g in range(N_GROUPS):
        v = sel[g * P:(g + 1) * P]
        best = None
        for a in range(P):
            for c in range(a + 1, P):
                s2 = v[a] + v[c]
                best = s2 if best is None else jnp.maximum(best, s2)
        gscore.append(best)
    gbest = jnp.zeros_like(gscore[0], dtype=jnp.int32)
    gmax = gscore[0]
    for g in range(1, N_GROUPS):
        better = gscore[g] > gmax
        gbest = jnp.where(better, g, gbest)
        gmax = jnp.where(better, gscore[g], gmax)

    def pick(vals):
        out = vals[0]
        for g in range(1, N_GROUPS):
            out = jnp.where(gbest == g, vals[g], out)
        return out

    sv = [pick([sel[g * P + j] for g in range(N_GROUPS)]) for j in range(P)]
    av = [pick([aff[g * P + j] for g in range(N_GROUPS)]) for j in range(P)]

    def first_max(vals, skip=None):
        idx = None
        best = None
        for j in range(P):
            v = vals[j] if skip is None else jnp.where(skip == j, -jnp.inf, vals[j])
            if best is None:
                idx, best = jnp.zeros_like(gbest), v
            else:
                better = v > best
                idx = jnp.where(better, j, idx)
                best = jnp.where(better, v, best)
        return idx

    j1 = first_max(sv)
    j2 = first_max(sv, skip=j1)
    g1 = av[0]
    g2 = av[0]
    for j in range(1, P):
        g1 = jnp.where(j1 == j, av[j], g1)
        g2 = jnp.where(j2 == j, av[j], g2)
    den = g1 + g2
    rows = []
    for e in range(N_EXPERTS):
        g, j = divmod(e, P)
        w = jnp.where(j1 == j, g1 / den, jnp.where(j2 == j, g2 / den, 0.0))
        rows.append(jnp.where(gbest == g, w, 0.0))
    o_ref[...] = jnp.concatenate(rows, axis=0)


def _router(x, w_routerT, bias, tm=512):
    T = x.shape[0]
    return pl.pallas_call(
        _router_kernel,
        grid=(T // tm,),
        in_specs=[pl.BlockSpec((tm, D_MODEL), lambda i: (i, 0)),
                  pl.BlockSpec((N_EXPERTS, D_MODEL), lambda i: (0, 0)),
                  pl.BlockSpec((N_EXPERTS, 1), lambda i: (0, 0))],
        out_specs=pl.BlockSpec((N_EXPERTS, tm), lambda i: (0, i)),
        out_shape=jax.ShapeDtypeStruct((N_EXPERTS, T), jnp.float32),
        compiler_params=_cparams("parallel"),
        name="router",
    )(x, w_routerT, bias)


def _moe_kernel(x_ref, c_ref, wg_ref, wu_ref, wd_ref, g_ref, b_ref, o_ref, xb_ref, acc_ref, *, alpha):
    e = pl.program_id(1)

    @pl.when(e == 0)
    def _():
        xb_ref[...] = x_ref[...].astype(jnp.bfloat16)
        acc_ref[...] = jnp.zeros_like(acc_ref)

    xb = xb_ref[...]
    gate = jnp.dot(xb, wg_ref[0], preferred_element_type=jnp.float32)
    up = jnp.dot(xb, wu_ref[0], preferred_element_type=jnp.float32)
    he = (jax.nn.silu(gate) * up).astype(jnp.bfloat16)
    dn = jnp.dot(he, wd_ref[0], preferred_element_type=jnp.float32)
    lane = lax.broadcasted_iota(jnp.int32, c_ref.shape, 1)
    ce = jnp.sum(jnp.where(lane == e, c_ref[...], 0.0), axis=1, keepdims=True)
    acc_ref[...] += ce * dn

    @pl.when(e == N_EXPERTS - 1)
    def _():
        o_ref[...] = _layer_norm(alpha * x_ref[...] + acc_ref[...], g_ref[...], b_ref[...])


def _moe(x, comb, w_gate, w_up, w_down, g, b, alpha, tm=512):
    T = x.shape[0]
    vec = pl.BlockSpec((1, D_MODEL), lambda i, e: (0, 0))
    return pl.pallas_call(
        functools.partial(_moe_kernel, alpha=alpha),
        grid=(T // tm, N_EXPERTS),
        in_specs=[pl.BlockSpec((tm, D_MODEL), lambda i, e: (i, 0)),
                  pl.BlockSpec((tm, N_EXPERTS), lambda i, e: (i, 0)),
                  pl.BlockSpec((1, D_MODEL, D_EXPERT), lambda i, e: (e, 0, 0)),
                  pl.BlockSpec((1, D_MODEL, D_EXPERT), lambda i, e: (e, 0, 0)),
                  pl.BlockSpec((1, D_EXPERT, D_MODEL), lambda i, e: (e, 0, 0)),
                  vec, vec],
        out_specs=pl.BlockSpec((tm, D_MODEL), lambda i, e: (i, 0)),
        out_shape=jax.ShapeDtypeStruct((T, D_MODEL), jnp.float32),
        scratch_shapes=[pltpu.VMEM((tm, D_MODEL), jnp.bfloat16),
                        pltpu.VMEM((tm, D_MODEL), jnp.float32)],
        compiler_params=_cparams("parallel", "arbitrary"),
        name="moe",
    )(x, comb, w_gate, w_up, w_down, g, b)


def _pack_w_in(w):
    xr, gr, q, ckv, qi, ki, wi, gate_r, gate_a = jnp.split(
        w, [1024, 2048, 3072, 3328, 3840, 3904, 3912, 4936], axis=1)
    pad = jnp.zeros((w.shape[0], D_IN_PAD - w.shape[1]), w.dtype)
    return jnp.concatenate([xr, gr, q, gate_r, gate_a, qi, ckv, ki, wi, pad], axis=1).astype(jnp.bfloat16)


def _block_diag_tiles(w):
    per = 256 // RNN_BW
    w = w.reshape(D_RNN // 256, per, RNN_BW, RNN_BW)
    eye = jnp.eye(per, dtype=w.dtype)
    return jnp.einsum('cgij,gh->cgihj', w, eye).reshape(D_RNN // 256, 256, 256).astype(jnp.bfloat16)


def kernel(x, w_in, conv_w, conv_b, lru_wa, lru_ba, lru_wx, lru_bx, lru_lambda, kv_norm, w_uk, w_uv, proj_rnn, proj_att, w_out, ln1_g, ln1_b, w_router, router_bias, exp_w_gate, exp_w_up, exp_w_down, ln2_g, ln2_b, rel_bias):
    B, S, D = x.shape
    T = B * S
    depth = w_in.shape[0]
    alpha = (2 * depth) ** 0.25
    n_keep = min(TOPK_MAX, S // 4)
    bf = jnp.bfloat16
    row = lambda v: v.reshape(1, -1)

    bias_hi = rel_bias.T.astype(bf)
    bias_lo = (rel_bias.T - bias_hi.astype(jnp.float32)).astype(bf)
    bias_tab = jnp.concatenate([bias_hi, bias_lo], axis=0)
    w_routerT = w_router.T
    rbias = router_bias.reshape(N_EXPERTS, 1)

    x = x.reshape(T, D)
    for l in range(depth):
        h_all = _inproj(x, _pack_w_in(w_in[l]))
        y_rnn = _rglru(h_all, B, S, conv_w[l], row(conv_b[l]), _block_diag_tiles(lru_wa[l]), row(lru_ba[l]),
                       _block_diag_tiles(lru_wx[l]), row(lru_bx[l]), row(lru_lambda[l]))
        table = _kvpack(h_all, row(kv_norm[l]))
        qlat = _qlat(h_all, jnp.swapaxes(w_uk[l], 1, 2).astype(bf))
        kidx = h_all[:, COL_KIDX:COL_KIDX + IDX_DIM].astype(bf).reshape(B, S, IDX_DIM)
        gathered = []
        for b in range(B):
            keys, lo = _indexer(h_all, kidx, b, S, n_keep)
            lo_w = lo[0, :, 0].reshape(S // SC_WORKERS, SC_WORKERS).T
            gathered.append(_select_gather(keys[0], lo_w, table, b * S, n_keep))
        w_uv_l = w_uv[l].astype(bf)
        y_att = jnp.concatenate(
            [_oproj(_attn(qlat, rows, idx.reshape(S, n_keep), bias_tab, b), w_uv_l)
             for b, (idx, rows) in enumerate(gathered)], axis=0)
        x = _merge(x, y_rnn, y_att, h_all, proj_rnn[l].astype(bf), proj_att[l].astype(bf),
                   w_out[l].astype(bf), row(ln1_g[l]), row(ln1_b[l]), alpha)
        comb = _router(x, w_routerT, rbias).T
        x = _moe(x, comb, exp_w_gate[l].astype(bf), exp_w_up[l].astype(bf), exp_w_down[l].astype(bf),
                 row(ln2_g[l]), row(ln2_b[l]), alpha)
    return x.reshape(B, S, D)
```

```python
import dataclasses
import functools
import math

import jax
import jax.numpy as jnp
from jax import lax
from jax.experimental import pallas as pl
from jax.experimental.pallas import tpu as pltpu
from jax.experimental.pallas import tpu_sc as plsc

D_MODEL = 1024
D_RNN = 1024
RNN_BLOCKS = 16
RNN_BW = D_RNN // RNN_BLOCKS
CONV_W = 4
LRU_C = 8.0
N_HEADS = 8
HEAD_DIM = 128
D_ATT = N_HEADS * HEAD_DIM
KV_RANK = 256
IDX_HEADS = 8
IDX_DIM = 64
TOPK_MAX = 256
N_BUCKETS = 32
MAX_DIST = 128
N_EXPERTS = 16
N_GROUPS = 4
EXPERTS_PER_GROUP = N_EXPERTS // N_GROUPS
D_EXPERT = 512
LN_EPS = 1e-5
RMS_EPS = 1e-6

COL_XRNN = 0
COL_GRNN = 1024
COL_Q = 2048
COL_GATE_RNN = 3072
COL_GATE_ATT = 4096
COL_QIDX = 5120
COL_CKV = 5632
COL_KIDX = 5888
D_IN_PAD = 6144

SC_CORES = 2
SC_SUBCORES = 16
SC_LANES = 16
SC_WORKERS = SC_CORES * SC_SUBCORES
SC_MAX_INDEX_ROW = 128

SEQ_PIECES = 4

INT_MIN = -2 ** 31
VMEM_LIMIT = 56 * 1024 * 1024


def _cparams(*sem):
    return pltpu.CompilerParams(dimension_semantics=sem, vmem_limit_bytes=VMEM_LIMIT)


def _layer_norm(v, g, b):
    mu = jnp.mean(v, axis=-1, keepdims=True)
    var = jnp.mean(jnp.square(v - mu), axis=-1, keepdims=True)
    return (v - mu) * lax.rsqrt(var + LN_EPS) * g + b


def _inproj_kernel(x_ref, w_ref, o_ref, xb_ref):
    @pl.when(pl.program_id(1) == 0)
    def _():
        xb_ref[...] = x_ref[...].astype(jnp.bfloat16)

    o_ref[...] = jnp.dot(xb_ref[...], w_ref[...], preferred_element_type=jnp.float32)


def _inproj(x, w, tm=1024, tn=512):
    T, K = x.shape
    N = w.shape[1]
    return pl.pallas_call(
        _inproj_kernel,
        grid=(T // tm, N // tn),
        in_specs=[pl.BlockSpec((tm, K), lambda i, j: (i, 0)),
                  pl.BlockSpec((K, tn), lambda i, j: (0, j))],
        out_specs=pl.BlockSpec((tm, tn), lambda i, j: (i, j)),
        out_shape=jax.ShapeDtypeStruct((T, N), jnp.float32),
        scratch_shapes=[pltpu.VMEM((tm, K), jnp.bfloat16)],
        compiler_params=_cparams("parallel", "arbitrary"),
        name="inproj",
    )(x, w)


def _rglru_kernel(x_ref, g_ref, cw_ref, cb_ref, wa_ref, ba_ref, wx_ref, bx_ref, lam_ref,
                  o_ref, prev_ref, carry_ref, a_ref, u_ref):
    ts = x_ref.shape[0]

    @pl.when(pl.program_id(1) == 0)
    def _():
        prev_ref[...] = jnp.zeros_like(prev_ref)
        carry_ref[...] = jnp.zeros_like(carry_ref)

    x = x_ref[...]
    xe = jnp.concatenate([prev_ref[...], x], axis=0)
    xr = cb_ref[...] + sum(cw_ref[k:k + 1, :] * xe[5 + k:5 + k + ts, :] for k in range(CONV_W))
    prev_ref[...] = x[ts - 8:, :]

    xb = xr.astype(jnp.bfloat16)
    nt = D_RNN // 256
    ra = jnp.concatenate([jnp.dot(xb[:, c * 256:(c + 1) * 256], wa_ref[c],
                                  preferred_element_type=jnp.float32) for c in range(nt)], axis=1)
    rx = jnp.concatenate([jnp.dot(xb[:, c * 256:(c + 1) * 256], wx_ref[c],
                                  preferred_element_type=jnp.float32) for c in range(nt)], axis=1)
    r = jax.nn.sigmoid(ra + ba_ref[...])
    gi = jax.nn.sigmoid(rx + bx_ref[...])
    z = -lam_ref[...]
    softplus = jnp.maximum(z, 0.0) + jnp.log(1.0 + jnp.exp(-jnp.abs(z)))
    log_a = (-LRU_C * r) * softplus
    a_ref[...] = jnp.exp(log_a)
    u_ref[...] = jnp.sqrt(1.0 - jnp.exp(2.0 * log_a)) * (gi * xr)

    row = lax.broadcasted_iota(jnp.int32, (8, D_RNN), 0)

    def group(gidx, carry):
        r0 = pl.multiple_of(gidx * 8, 8)
        a8 = a_ref[pl.ds(r0, 8), :]
        u8 = u_ref[pl.ds(r0, 8), :]
        for d in (1, 2, 4):
            keep = row >= d
            a_sh = pltpu.roll(a8, d, 0)
            u_sh = pltpu.roll(u8, d, 0)
            u8 = jnp.where(keep, a8 * u_sh + u8, u8)
            a8 = jnp.where(keep, a8 * a_sh, a8)
        h8 = a8 * carry + u8
        u_ref[pl.ds(r0, 8), :] = h8
        return h8[7:8, :]

    carry_ref[...] = lax.fori_loop(0, ts // 8, group, carry_ref[...], unroll=4)
    o_ref[...] = (u_ref[...] * jax.nn.gelu(g_ref[...])).astype(o_ref.dtype)


def _rglru(h_all, B, S, cw, cb, wa, ba, wx, bx, lam, ts=256):
    nblk = S // ts
    row = lambda b, i: b * nblk + i
    vec = pl.BlockSpec((1, D_RNN), lambda b, i: (0, 0))
    tile = pl.BlockSpec((D_RNN // 256, 256, 256), lambda b, i: (0, 0, 0))
    return pl.pallas_call(
        _rglru_kernel,
        grid=(B, nblk),
        in_specs=[pl.BlockSpec((ts, D_RNN), lambda b, i: (row(b, i), COL_XRNN // D_RNN)),
                  pl.BlockSpec((ts, D_RNN), lambda b, i: (row(b, i), COL_GRNN // D_RNN)),
                  pl.BlockSpec((CONV_W, D_RNN), lambda b, i: (0, 0)),
                  vec, tile, vec, tile, vec, vec],
        out_specs=pl.BlockSpec((ts, D_RNN), lambda b, i: (row(b, i), 0)),
        out_shape=jax.ShapeDtypeStruct((B * S, D_RNN), jnp.bfloat16),
        scratch_shapes=[pltpu.VMEM((8, D_RNN), jnp.float32),
                        pltpu.VMEM((1, D_RNN), jnp.float32),
                        pltpu.VMEM((ts, D_RNN), jnp.float32),
                        pltpu.VMEM((ts, D_RNN), jnp.float32)],
        compiler_params=_cparams("arbitrary", "arbitrary"),
        name="rglru",
    )(h_all, h_all, cw, cb, wa, ba, wx, bx, lam)


def _kvpack_kernel(c_ref, g_ref, o_ref):
    c = c_ref[...]
    cn = c * lax.rsqrt(jnp.mean(jnp.square(c), axis=-1, keepdims=True) + RMS_EPS) * g_ref[...]
    cb = cn.astype(jnp.bfloat16).astype(jnp.float32)
    half = KV_RANK // 2
    lo = lax.bitcast_convert_type(cb[:, :half], jnp.int32)
    hi = lax.bitcast_convert_type(cb[:, half:], jnp.int32)
    o_ref[...] = (hi & jnp.int32(-65536)) | lax.shift_right_logical(lo, 16)


def _kvpack(h_all, kv_norm, ts=1024):
    T = h_all.shape[0]
    return pl.pallas_call(
        _kvpack_kernel,
        grid=(T // ts,),
        in_specs=[pl.BlockSpec((ts, KV_RANK), lambda i: (i, COL_CKV // KV_RANK)),
                  pl.BlockSpec((1, KV_RANK), lambda i: (0, 0))],
        out_specs=pl.BlockSpec((ts, KV_RANK // 2), lambda i: (i, 0)),
        out_shape=jax.ShapeDtypeStruct((T, KV_RANK // 2), jnp.int32),
        compiler_params=_cparams("parallel"),
        name="kvpack",
    )(h_all, kv_norm)


def _unpack_kv(w):
    lo = lax.bitcast_convert_type(lax.shift_left(w, 16), jnp.float32)
    hi = lax.bitcast_convert_type(w & jnp.int32(-65536), jnp.float32)
    return jnp.concatenate([lo, hi], axis=-1).astype(jnp.bfloat16)


def _qlat_kernel(q_ref, w_ref, o_ref):
    q = q_ref[...].astype(jnp.bfloat16)
    for h in range(N_HEADS):
        o_ref[:, h, :] = jnp.dot(q[:, h * HEAD_DIM:(h + 1) * HEAD_DIM], w_ref[h],
                                 preferred_element_type=jnp.float32) * (HEAD_DIM ** -0.5)


def _qlat(h_all, w_ukT, tm=256):
    T = h_all.shape[0]
    return pl.pallas_call(
        _qlat_kernel,
        grid=(T // tm,),
        in_specs=[pl.BlockSpec((tm, D_ATT), lambda i: (i, COL_Q // D_ATT)),
                  pl.BlockSpec((N_HEADS, HEAD_DIM, KV_RANK), lambda i: (0, 0, 0))],
        out_specs=pl.BlockSpec((tm, N_HEADS, KV_RANK), lambda i: (i, 0, 0)),
        out_shape=jax.ShapeDtypeStruct((T, N_HEADS, KV_RANK), jnp.float32),
        compiler_params=_cparams("parallel"),
        name="qlat",
    )(h_all, w_ukT)


IDX_QB = 128
IDX_KC = 512
IDX_TOPS = 2


def _indexer_kernel(q_ref, kw_ref, k_ref, keys_ref, lo_ref, *, n_keep, blk0):
    S = keys_ref.shape[2]
    i = pl.program_id(0) + blk0
    n_causal = (i * IDX_QB + IDX_QB - 1) // IDX_KC + 1
    q_all = jnp.concatenate(
        [(q_ref[:, h * IDX_DIM:(h + 1) * IDX_DIM] * (IDX_DIM ** -0.5)).astype(jnp.bfloat16)
         for h in range(IDX_HEADS)], axis=0)
    ws = [kw_ref[:, IDX_DIM + h:IDX_DIM + h + 1] * (IDX_HEADS ** -0.5) for h in range(IDX_HEADS)]
    t = i * IDX_QB + lax.broadcasted_iota(jnp.int32, (IDX_QB, IDX_KC), 0)
    lane = lax.broadcasted_iota(jnp.int32, (IDX_QB, IDX_KC), 1)
    n_slab = IDX_KC // 128

    def score_chunk(c, carry):
        tops = list(carry)
        c0 = pl.multiple_of(c * IDX_KC, IDX_KC)
        kc = k_ref[0, pl.ds(c0, IDX_KC), :]
        d = lax.dot_general(q_all, kc, (((1,), (1,)), ((), ())), preferred_element_type=jnp.float32)
        sc = jnp.zeros((IDX_QB, IDX_KC), jnp.float32)
        for h in range(IDX_HEADS):
            sc = sc + jnp.maximum(d[h * IDX_QB:(h + 1) * IDX_QB], 0.0) * ws[h]
        bits = lax.bitcast_convert_type(sc, jnp.int32)
        key = bits ^ (lax.shift_right_arithmetic(bits, 31) & jnp.int32(0x7FFFFFFF))
        key = jnp.where(c0 + lane <= t, key, jnp.int32(INT_MIN))
        keys_ref[0, :, pl.ds(c0, IDX_KC)] = key
        for j in range(n_slab):
            x = key[:, j * 128:(j + 1) * 128]
            for r in range(IDX_TOPS - 1, 0, -1):
                tops[r] = jnp.maximum(tops[r], jnp.minimum(tops[r - 1], x))
            tops[0] = jnp.maximum(tops[0], x)
        return tuple(tops)

    floor = jnp.full((IDX_QB, 128), INT_MIN, jnp.int32)
    tops = lax.fori_loop(0, n_causal, score_chunk, (floor,) * IDX_TOPS)

    def blank_chunk(c, _):
        c0 = pl.multiple_of(c * IDX_KC, IDX_KC)
        keys_ref[0, :, pl.ds(c0, IDX_KC)] = jnp.full((IDX_QB, IDX_KC), INT_MIN, jnp.int32)
        return 0

    lax.fori_loop(n_causal, S // IDX_KC, blank_chunk, 0)

    def bit_pass(p, res):
        cand = res | lax.shift_left(jnp.int32(1), 31 - p)
        cand_b = jnp.broadcast_to(cand ^ jnp.int32(INT_MIN), (IDX_QB, 128))
        hits = sum(jnp.where(tr >= cand_b, 1, 0) for tr in tops)
        return jnp.where(jnp.sum(hits, axis=1, keepdims=True) >= n_keep, cand, res)

    if n_keep == 128 * IDX_TOPS:
        bound = jnp.min(tops[-1], axis=1, keepdims=True)
    else:
        bound = lax.fori_loop(0, 32, bit_pass, jnp.zeros((IDX_QB, 1), jnp.int32)) ^ jnp.int32(INT_MIN)
    lo = jnp.maximum(bound, jnp.int32(INT_MIN + 1))
    lo_ref[0] = jnp.broadcast_to(lo, (IDX_QB, 128))


def _indexer(h_all, kidx, b, t_off, n_q, S, n_keep):
    assert n_keep <= 128 * IDX_TOPS
    blk0 = t_off // IDX_QB
    first = b * (S // IDX_QB) + blk0
    return pl.pallas_call(
        functools.partial(_indexer_kernel, n_keep=n_keep, blk0=blk0),
        grid=(n_q // IDX_QB,),
        in_specs=[pl.BlockSpec((IDX_QB, IDX_HEADS * IDX_DIM), lambda i: (first + i, COL_QIDX // 512)),
                  pl.BlockSpec((IDX_QB, 128), lambda i: (first + i, COL_KIDX // 128)),
                  pl.BlockSpec((1, S, IDX_DIM), lambda i: (b, 0, 0))],
        out_specs=[pl.BlockSpec((1, IDX_QB, S), lambda i: (0, i, 0)),
                   pl.BlockSpec((1, IDX_QB, 128), lambda i: (0, i, 0))],
        out_shape=[jax.ShapeDtypeStruct((1, n_q, S), jnp.int32),
                   jax.ShapeDtypeStruct((1, n_q, 128), jnp.int32)],
        compiler_params=_cparams("parallel"),
        name="indexer",
    )(h_all, h_all, kidx)


SC_SEG = 4096
SC_VALUE_STEPS = 40


def _select_gather(keys, lo_w, table, base, t_off, n_keep):
    T, S = keys.shape
    W = table.shape[1]
    L = SC_LANES
    per_worker = T // SC_WORKERS
    n_idx_rows = n_keep // SC_MAX_INDEX_ROW
    mesh = plsc.VectorSubcoreMesh(core_axis_name="c", subcore_axis_name="s")
    cp = pltpu.CompilerParams()
    if "needs_layout_passes" in pltpu.CompilerParams.__dataclass_fields__:
        cp = dataclasses.replace(cp, needs_layout_passes=False)

    seg = min(SC_SEG, S)
    n_seg = S // seg
    unroll = 8
    per_row = SC_MAX_INDEX_ROW // L

    @functools.partial(
        pl.kernel, mesh=mesh, compiler_params=cp,
        out_type=(jax.ShapeDtypeStruct((T, n_idx_rows, SC_MAX_INDEX_ROW), jnp.int32),
                  jax.ShapeDtypeStruct((T, n_keep, W), jnp.int32)),
        scratch_types=[pltpu.VMEM((2, S), jnp.int32),
                       pltpu.VMEM((per_worker,), jnp.int32),
                       pltpu.VMEM((S,), jnp.int32),
                       pltpu.VMEM((n_idx_rows, SC_MAX_INDEX_ROW), jnp.int32),
                       pltpu.VMEM((2, n_idx_rows, SC_MAX_INDEX_ROW), jnp.int32),
                       pltpu.VMEM((2, n_keep, W), jnp.int32),
                       pltpu.SemaphoreType.DMA((2,)),
                       pltpu.SemaphoreType.DMA((2,)),
                       pltpu.SemaphoreType.DMA((2,))],
        name="select_gather",
    )
    def body(keys_hbm, lo_hbm, table_hbm, idx_hbm, rows_hbm,
             krow, lo_v, cand, idx_s, idx_g, rows_v, key_sem, gat_sem, out_sem):
        wid = lax.axis_index("s") * SC_CORES + lax.axis_index("c")
        lane = lax.iota(jnp.int32, L)
        zero = jnp.zeros((L,), jnp.int32)

        def key_copy(r, k, buf):
            return pltpu.make_async_copy(keys_hbm.at[r, pl.ds(k * seg, seg)],
                                         krow.at[buf, pl.ds(k * seg, seg)], key_sem.at[buf])

        def gather_copy(h, buf):
            return pltpu.make_async_copy(
                table_hbm.at[idx_g.at[buf, h]],
                rows_v.at[buf, pl.ds(h * SC_MAX_INDEX_ROW, SC_MAX_INDEX_ROW)], gat_sem.at[buf])

        def out_copies(r, buf):
            return (pltpu.make_async_copy(rows_v.at[buf], rows_hbm.at[r], out_sem.at[buf]),
                    pltpu.make_async_copy(idx_g.at[buf], idx_hbm.at[r], out_sem.at[buf]))

        def row_of(q):
            r = wid + SC_WORKERS * q
            return r, r + t_off, base

        def fetch_keys(q, buf):
            r, t, _ = row_of(q)
            for k in range(n_seg):
                @pl.when((q < per_worker) & (t >= k * seg))
                def _():
                    key_copy(r, k, buf).start()

        def step(q, buf):
            r, t, row0 = row_of(q)
            fetch_keys(q + 1, 1 - buf)
            for k in range(n_seg):
                @pl.when(t >= k * seg)
                def _():
                    key_copy(r, k, buf).wait()
            lo0 = plsc.load_gather(lo_v, [zero + q])
            fill = zero + row0
            for j in range(n_keep // L):
                idx_s[j // per_row, pl.ds((j % per_row) * L, L)] = fill

            def ones(m):
                return jnp.where(m, 1, 0).astype(jnp.int32)

            def pick(g, carry):
                cnt, top = carry
                vs = [krow[buf, pl.ds((g * unroll + u) * L, L)] for u in range(unroll)]
                for u, v in enumerate(vs):
                    m = v >= lo0
                    pos = cnt + plsc.cumsum(ones(m)) - 1
                    plsc.store_scatter(cand, [pos], (g * unroll + u) * L + lane, mask=m)
                    cnt = cnt + plsc.all_reduce_population_count(m)
                    top = jnp.maximum(top, v)
                return cnt, top

            n_cand, top = plsc.parallel_loop(0, t // (unroll * L) + 1,
                                             carry=(zero, zero + INT_MIN))(pick)
            n_vec = (jnp.max(n_cand) + L - 1) // L

            def pack(j, _):
                ok = j * L + lane < n_cand
                pos = jnp.where(ok, cand[pl.ds(j * L, L)], 0)
                krow[buf, pl.ds(j * L, L)] = plsc.load_gather(krow, [zero + buf, pos])
                return 0

            lax.fori_loop(0, n_vec, pack, 0)

            def cand_keys(j):
                ok = j * L + lane < n_cand
                return krow[buf, pl.ds(j * L, L)], cand[pl.ds(j * L, L)], ok

            count_unroll = 4

            def count(pred):
                def some(g, acc):
                    for u in range(count_unroll):
                        kv, _, ok = cand_keys(g * count_unroll + u)
                        acc = acc + plsc.all_reduce_population_count(pred(kv) & ok)
                    return acc
                return plsc.parallel_loop(0, (n_vec + count_unroll - 1) // count_unroll, carry=zero)(some)

            search = n_cand > n_keep

            def active(lo, hi, cnt):
                return search & (cnt != n_keep) & (lo + 1 < hi)

            def flip(v):
                return v ^ (lax.shift_right_arithmetic(v, 31) & jnp.int32(0x7FFFFFFF))

            def probe(lo, hi, step):
                mid = (lo & hi) + lax.shift_right_arithmetic(lo ^ hi, 1)
                mean = (0.5 * lax.bitcast_convert_type(flip(lo), jnp.float32)
                        + 0.5 * lax.bitcast_convert_type(flip(hi), jnp.float32))
                mid_v = flip(lax.bitcast_convert_type(mean, jnp.int32))
                width = hi - lo
                far = (width < 0) | (width > (1 << 24))
                mid = jnp.where(far & (step < SC_VALUE_STEPS) & (mid_v > lo) & (mid_v < hi), mid_v, mid)
                mid = jnp.where((lo < 0) & (hi > 0), 0, mid)
                return jnp.where((lo == 0) & (hi > 1), 1, mid)

            def search_body(carry):
                lo, hi, cnt, step, _ = carry
                mid = probe(lo, hi, step)
                n = count(lambda kv: kv >= mid)
                up = n >= n_keep
                lo, hi, cnt = jnp.where(up, mid, lo), jnp.where(up, hi, mid), jnp.where(up, n, cnt)
                return lo, hi, cnt, step + 1, jnp.max(ones(active(lo, hi, cnt)))

            hi0 = zero + jnp.max(top) + 1
            lo, hi, cnt, _, _ = lax.while_loop(
                lambda carry: carry[4] > 0, search_body,
                (lo0, hi0, n_cand, jnp.int32(0), jnp.max(ones(active(lo0, hi0, n_cand)))))

            tie = search & (cnt != n_keep)
            thr = jnp.where(tie, lo, lo - 1)
            n_eq = jnp.where(tie, n_keep - count(lambda kv: kv > lo), 0)

            def emit(j, carry):
                cnt, seen = carry
                kv, pos, ok = cand_keys(j)
                eq = (kv == thr) & ok
                rank = seen + plsc.cumsum(ones(eq))
                m = ok & ((kv > thr) | (eq & (rank <= n_eq)))
                slot = cnt + plsc.cumsum(ones(m)) - 1
                fits = m & (slot < n_keep)
                slot = jnp.minimum(slot, n_keep - 1)
                plsc.store_scatter(idx_s, [slot >> 7, slot & 127], pos + row0, mask=fits)
                return (cnt + plsc.all_reduce_population_count(m),
                        seen + plsc.all_reduce_population_count(eq))

            lax.fori_loop(0, n_vec, emit, (zero, zero))

            @pl.when(q >= 2)
            def _():
                for c in out_copies(r, buf):
                    c.wait()

            for j in range(n_keep // L):
                sl = (j // per_row, pl.ds((j % per_row) * L, L))
                idx_g[(buf,) + sl] = idx_s[sl]

            @pl.when(q >= 1)
            def _():
                for h in range(n_idx_rows):
                    gather_copy(h, 1 - buf).wait()
                for c in out_copies(r - SC_WORKERS, 1 - buf):
                    c.start()

            for h in range(n_idx_rows):
                gather_copy(h, buf).start()

        pltpu.sync_copy(lo_hbm.at[wid], lo_v)
        fetch_keys(jnp.int32(0), 0)

        def pair(p, _):
            step(2 * p, 0)
            step(2 * p + 1, 1)
            return 0

        lax.fori_loop(0, per_worker // 2, pair, 0)

        r_last = row_of(per_worker - 1)[0]
        for h in range(n_idx_rows):
            gather_copy(h, 1).wait()
        for c in out_copies(r_last, 1):
            c.start()
        for c in out_copies(r_last, 0) + out_copies(r_last, 1):
            c.wait()

    return body(keys, lo_w, table)


ATT_GROUP = 8


def _attn_kernel(ql_ref, kv_ref, idx_ref, tbl_ref, o_ref, kvb_ref, lg_ref, bk_ref, *, base, t_off):
    tq = ql_ref.shape[0]
    n_keep = kv_ref.shape[1]
    t0 = t_off + pl.program_id(0) * tq
    row0 = base
    max_exact = N_BUCKETS // 2

    t_row = t0 + lax.broadcasted_iota(jnp.int32, (tq, n_keep), 0)
    d = jnp.maximum(t_row - (idx_ref[...] - row0), 0)
    large = max_exact + (jnp.log(jnp.maximum(d, 1).astype(jnp.float32) / max_exact)
                         / math.log(MAX_DIST / max_exact) * (N_BUCKETS - max_exact)).astype(jnp.int32)
    bk_ref[...] = jnp.where(d < max_exact, d, jnp.minimum(large, N_BUCKETS - 1))
    bucket_id = lax.broadcasted_iota(jnp.int32, (N_BUCKETS, ATT_GROUP * n_keep), 0)

    G = ATT_GROUP

    def logits_of(g, _):
        q0 = pl.multiple_of(g * G, G)
        kv = _unpack_kv(kv_ref[pl.ds(q0, G)])
        kvb_ref[pl.ds(q0, G)] = kv
        buckets = jnp.concatenate([bk_ref[pl.ds(q0 + i, 1), :] for i in range(G)], axis=1)
        onehot = jnp.where(buckets == bucket_id, 1.0, 0.0).astype(jnp.bfloat16)
        bias2 = jnp.dot(tbl_ref[...], onehot, preferred_element_type=jnp.float32)
        bias = bias2[:N_HEADS] + bias2[N_HEADS:]
        ql = ql_ref[pl.ds(q0, G)].reshape(G * N_HEADS, KV_RANK).astype(jnp.bfloat16)
        full = lax.dot_general(ql, kv.reshape(G * n_keep, KV_RANK), (((1,), (1,)), ((), ())),
                               preferred_element_type=jnp.float32)
        for i in range(G):
            cols = slice(i * n_keep, (i + 1) * n_keep)
            lg_ref[q0 + i] = full[i * N_HEADS:(i + 1) * N_HEADS, cols] + bias[:, cols]
        return 0

    lax.fori_loop(0, tq // G, logits_of, 0, unroll=2)

    shape = (tq, N_HEADS, n_keep)
    valid = lax.broadcasted_iota(jnp.int32, shape, 2) <= t0 + lax.broadcasted_iota(jnp.int32, shape, 0)
    logits = jnp.where(valid, lg_ref[...], -1e30)
    e = jnp.exp(logits - jnp.max(logits, axis=-1, keepdims=True))
    lg_ref[...] = e / jnp.sum(e, axis=-1, keepdims=True)

    def values_of(g, _):
        q0 = pl.multiple_of(g * G, G)
        p = lg_ref[pl.ds(q0, G)]
        nothing = jnp.zeros((N_HEADS, n_keep), jnp.float32)
        p_diag = jnp.concatenate(
            [jnp.concatenate([p[i] if j == i else nothing for j in range(G)], axis=1) for i in range(G)],
            axis=0)
        out = jnp.dot(p_diag.astype(jnp.bfloat16), kvb_ref[pl.ds(q0, G)].reshape(G * n_keep, KV_RANK),
                      preferred_element_type=jnp.float32)
        o_ref[pl.ds(q0, G)] = out.reshape(G, N_HEADS, KV_RANK)
        return 0

    lax.fori_loop(0, tq // G, values_of, 0, unroll=2)


def _attn(qlat, rows, idx, bias_tab, base, t_off, tq=32):
    T, n_keep, W = rows.shape
    first = (base + t_off) // tq
    return pl.pallas_call(
        functools.partial(_attn_kernel, base=base, t_off=t_off),
        grid=(T // tq,),
        in_specs=[pl.BlockSpec((tq, N_HEADS, KV_RANK), lambda i: (first + i, 0, 0)),
                  pl.BlockSpec((tq, n_keep, W), lambda i: (i, 0, 0)),
                  pl.BlockSpec((tq, n_keep), lambda i: (i, 0)),
                  pl.BlockSpec((2 * N_HEADS, N_BUCKETS), lambda i: (0, 0))],
        out_specs=pl.BlockSpec((tq, N_HEADS, KV_RANK), lambda i: (i, 0, 0)),
        out_shape=jax.ShapeDtypeStruct((T, N_HEADS, KV_RANK), jnp.float32),
        scratch_shapes=[pltpu.VMEM((tq, n_keep, KV_RANK), jnp.bfloat16),
                        pltpu.VMEM((tq, N_HEADS, n_keep), jnp.float32),
                        pltpu.VMEM((tq, n_keep), jnp.int32)],
        compiler_params=_cparams("parallel"),
        name="attn",
    )(qlat, rows, idx, bias_tab)


def _oproj_kernel(o_ref, w_ref, y_ref):
    for h in range(N_HEADS):
        y_ref[:, h * HEAD_DIM:(h + 1) * HEAD_DIM] = jnp.dot(
            o_ref[:, h, :].astype(jnp.bfloat16), w_ref[h],
            preferred_element_type=jnp.float32).astype(y_ref.dtype)


def _oproj(o_lat, w_uv, tm=256):
    T = o_lat.shape[0]
    return pl.pallas_call(
        _oproj_kernel,
        grid=(T // tm,),
        in_specs=[pl.BlockSpec((tm, N_HEADS, KV_RANK), lambda i: (i, 0, 0)),
                  pl.BlockSpec((N_HEADS, KV_RANK, HEAD_DIM), lambda i: (0, 0, 0))],
        out_specs=pl.BlockSpec((tm, D_ATT), lambda i: (i, 0)),
        out_shape=jax.ShapeDtypeStruct((T, D_ATT), jnp.bfloat16),
        compiler_params=_cparams("parallel"),
        name="oproj",
    )(o_lat, w_uv)


def _merge_kernel(x_ref, yr_ref, ya_ref, gr_ref, ga_ref, pr_ref, pa_ref, wo_ref, g_ref, b_ref,
                  o_ref, *, alpha):
    a = jnp.dot(yr_ref[...], pr_ref[...], preferred_element_type=jnp.float32)
    c = jnp.dot(ya_ref[...], pa_ref[...], preferred_element_type=jnp.float32)
    merged = jax.nn.sigmoid(gr_ref[...]) * a + jax.nn.sigmoid(ga_ref[...]) * c
    mix = jnp.dot(merged.astype(jnp.bfloat16), wo_ref[...], preferred_element_type=jnp.float32)
    o_ref[...] = _layer_norm(alpha * x_ref[...] + mix, g_ref[...], b_ref[...])


def _merge(x, y_rnn, y_att, h_all, proj_rnn, proj_att, w_out, g, b, alpha, tm=512):
    T = x.shape[0]
    tok = lambda c: pl.BlockSpec((tm, D_MODEL), lambda i: (i, c))
    wgt = pl.BlockSpec((D_MODEL, D_MODEL), lambda i: (0, 0))
    vec = pl.BlockSpec((1, D_MODEL), lambda i: (0, 0))
    return pl.pallas_call(
        functools.partial(_merge_kernel, alpha=alpha),
        grid=(T // tm,),
        in_specs=[tok(0), tok(0), tok(0), tok(COL_GATE_RNN // D_MODEL), tok(COL_GATE_ATT // D_MODEL),
                  wgt, wgt, wgt, vec, vec],
        out_specs=tok(0),
        out_shape=jax.ShapeDtypeStruct((T, D_MODEL), jnp.float32),
        compiler_params=_cparams("parallel"),
        name="merge",
    )(x, y_rnn, y_att, h_all, h_all, proj_rnn, proj_att, w_out, g, b)


def _router_kernel(x_ref, w_ref, b_ref, o_ref):
    logits = lax.dot_general(w_ref[...], x_ref[...], (((1,), (1,)), ((), ())),
                             precision=lax.Precision.HIGHEST,
                             preferred_element_type=jnp.float32)
    aff = [jax.nn.sigmoid(logits[e:e + 1, :]) for e in range(N_EXPERTS)]
    sel = [aff[e] + b_ref[e:e + 1, :] for e in range(N_EXPERTS)]
    P = EXPERTS_PER_GROUP
    gscore = []
    for g in range(N_GROUPS):
        v = sel[g * P:(g + 1) * P]
        best = None
        for a in range(P):
            for c in range(a + 1, P):
                s2 = v[a] + v[c]
                best = s2 if best is None else jnp.maximum(best, s2)
        gscore.append(best)
    gbest = jnp.zeros_like(gscore[0], dtype=jnp.int32)
    gmax = gscore[0]
    for g in range(1, N_GROUPS):
        better = gscore[g] > gmax
        gbest = jnp.where(better, g, gbest)
        gmax = jnp.where(better, gscore[g], gmax)

    def pick(vals):
        out = vals[0]
        for g in range(1, N_GROUPS):
            out = jnp.where(gbest == g, vals[g], out)
        return out

    sv = [pick([sel[g * P + j] for g in range(N_GROUPS)]) for j in range(P)]
    av = [pick([aff[g * P + j] for g in range(N_GROUPS)]) for j in range(P)]

    def first_max(vals, skip=None):
        idx = None
        best = None
        for j in range(P):
            v = vals[j] if skip is None else jnp.where(skip == j, -jnp.inf, vals[j])
            if best is None:
                idx, best = jnp.zeros_like(gbest), v
            else:
                better = v > best
                idx = jnp.where(better, j, idx)
                best = jnp.where(better, v, best)
        return idx

    j1 = first_max(sv)
    j2 = first_max(sv, skip=j1)
    g1 = av[0]
    g2 = av[0]
    for j in range(1, P):
        g1 = jnp.where(j1 == j, av[j], g1)
        g2 = jnp.where(j2 == j, av[j], g2)
    den = g1 + g2
    rows = []
    for e in range(N_EXPERTS):
        g, j = divmod(e, P)
        w = jnp.where(j1 == j, g1 / den, jnp.where(j2 == j, g2 / den, 0.0))
        rows.append(jnp.where(gbest == g, w, 0.0))
    o_ref[...] = jnp.concatenate(rows, axis=0)


def _router(x, w_routerT, bias, tm=512):
    T = x.shape[0]
    return pl.pallas_call(
        _router_kernel,
        grid=(T // tm,),
        in_specs=[pl.BlockSpec((tm, D_MODEL), lambda i: (i, 0)),
                  pl.BlockSpec((N_EXPERTS, D_MODEL), lambda i: (0, 0)),
                  pl.BlockSpec((N_EXPERTS, 1), lambda i: (0, 0))],
        out_specs=pl.BlockSpec((N_EXPERTS, tm), lambda i: (0, i)),
        out_shape=jax.ShapeDtypeStruct((N_EXPERTS, T), jnp.float32),
        compiler_params=_cparams("parallel"),
        name="router",
    )(x, w_routerT, bias)


def _moe_kernel(x_ref, c_ref, wg_ref, wu_ref, wd_ref, g_ref, b_ref, o_ref, xb_ref, acc_ref, *, alpha):
    e = pl.program_id(1)

    @pl.when(e == 0)
    def _():
        xb_ref[...] = x_ref[...].astype(jnp.bfloat16)
        acc_ref[...] = jnp.zeros_like(acc_ref)

    xb = xb_ref[...]
    gate = jnp.dot(xb, wg_ref[0], preferred_element_type=jnp.float32)
    up = jnp.dot(xb, wu_ref[0], preferred_element_type=jnp.float32)
    he = (jax.nn.silu(gate) * up).astype(jnp.bfloat16)
    dn = jnp.dot(he, wd_ref[0], preferred_element_type=jnp.float32)
    lane = lax.broadcasted_iota(jnp.int32, c_ref.shape, 1)
    ce = jnp.sum(jnp.where(lane == e, c_ref[...], 0.0), axis=1, keepdims=True)
    acc_ref[...] += ce * dn

    @pl.when(e == N_EXPERTS - 1)
    def _():
        o_ref[...] = _layer_norm(alpha * x_ref[...] + acc_ref[...], g_ref[...], b_ref[...])


def _moe(x, comb, w_gate, w_up, w_down, g, b, alpha, tm=512):
    T = x.shape[0]
    vec = pl.BlockSpec((1, D_MODEL), lambda i, e: (0, 0))
    return pl.pallas_call(
        functools.partial(_moe_kernel, alpha=alpha),
        grid=(T // tm, N_EXPERTS),
        in_specs=[pl.BlockSpec((tm, D_MODEL), lambda i, e: (i, 0)),
                  pl.BlockSpec((tm, N_EXPERTS), lambda i, e: (i, 0)),
                  pl.BlockSpec((1, D_MODEL, D_EXPERT), lambda i, e: (e, 0, 0)),
                  pl.BlockSpec((1, D_MODEL, D_EXPERT), lambda i, e: (e, 0, 0)),
                  pl.BlockSpec((1, D_EXPERT, D_MODEL), lambda i, e: (e, 0, 0)),
                  vec, vec],
        out_specs=pl.BlockSpec((tm, D_MODEL), lambda i, e: (i, 0)),
        out_shape=jax.ShapeDtypeStruct((T, D_MODEL), jnp.float32),
        scratch_shapes=[pltpu.VMEM((tm, D_MODEL), jnp.bfloat16),
                        pltpu.VMEM((tm, D_MODEL), jnp.float32)],
        compiler_params=_cparams("parallel", "arbitrary"),
        name="moe",
    )(x, comb, w_gate, w_up, w_down, g, b)


def _pack_w_in(w):
    xr, gr, q, ckv, qi, ki, wi, gate_r, gate_a = jnp.split(
        w, [1024, 2048, 3072, 3328, 3840, 3904, 3912, 4936], axis=1)
    pad = jnp.zeros((w.shape[0], D_IN_PAD - w.shape[1]), w.dtype)
    return jnp.concatenate([xr, gr, q, gate_r, gate_a, qi, ckv, ki, wi, pad], axis=1).astype(jnp.bfloat16)


def _block_diag_tiles(w):
    per = 256 // RNN_BW
    w = w.reshape(D_RNN // 256, per, RNN_BW, RNN_BW)
    eye = jnp.eye(per, dtype=w.dtype)
    return jnp.einsum('cgij,gh->cgihj', w, eye).reshape(D_RNN // 256, 256, 256).astype(jnp.bfloat16)


def kernel(x, w_in, conv_w, conv_b, lru_wa, lru_ba, lru_wx, lru_bx, lru_lambda, kv_norm, w_uk, w_uv, proj_rnn, proj_att, w_out, ln1_g, ln1_b, w_router, router_bias, exp_w_gate, exp_w_up, exp_w_down, ln2_g, ln2_b, rel_bias):
    B, S, D = x.shape
    T = B * S
    depth = w_in.shape[0]
    alpha = (2 * depth) ** 0.25
    n_keep = min(TOPK_MAX, S // 4)
    bf = jnp.bfloat16
    row = lambda v: v.reshape(1, -1)

    bias_hi = rel_bias.T.astype(bf)
    bias_lo = (rel_bias.T - bias_hi.astype(jnp.float32)).astype(bf)
    bias_tab = jnp.concatenate([bias_hi, bias_lo], axis=0)
    w_routerT = w_router.T
    rbias = router_bias.reshape(N_EXPERTS, 1)

    x = x.reshape(T, D)
    for l in range(depth):
        h_all = _inproj(x, _pack_w_in(w_in[l]))
        y_rnn = _rglru(h_all, B, S, conv_w[l], row(conv_b[l]), _block_diag_tiles(lru_wa[l]), row(lru_ba[l]),
                       _block_diag_tiles(lru_wx[l]), row(lru_bx[l]), row(lru_lambda[l]))
        table = _kvpack(h_all, row(kv_norm[l]))
        qlat = _qlat(h_all, jnp.swapaxes(w_uk[l], 1, 2).astype(bf))
        kidx = h_all[:, COL_KIDX:COL_KIDX + IDX_DIM].astype(bf).reshape(B, S, IDX_DIM)
        n_q = S // SEQ_PIECES
        pieces = [(b, part * n_q) for b in range(B) for part in range(SEQ_PIECES)]
        gathered = []
        for b, t_off in pieces:
            keys, lo = _indexer(h_all, kidx, b, t_off, n_q, S, n_keep)
            lo_w = lo[0, :, 0].reshape(n_q // SC_WORKERS, SC_WORKERS).T
            gathered.append(_select_gather(keys[0], lo_w, table, b * S, t_off, n_keep))
        w_uv_l = w_uv[l].astype(bf)
        y_att = jnp.concatenate(
            [_oproj(_attn(qlat, rows, idx.reshape(n_q, n_keep), bias_tab, b * S, t_off), w_uv_l)
             for (b, t_off), (idx, rows) in zip(pieces, gathered)], axis=0)
        x = _merge(x, y_rnn, y_att, h_all, proj_rnn[l].astype(bf), proj_att[l].astype(bf),
                   w_out[l].astype(bf), row(ln1_g[l]), row(ln1_b[l]), alpha)
        comb = _router(x, w_routerT, rbias).T
        x = _moe(x, comb, exp_w_gate[l].astype(bf), exp_w_up[l].astype(bf), exp_w_down[l].astype(bf),
                 row(ln2_g[l]), row(ln2_b[l]), alpha)
    return x.reshape(B, S, D)
```
